```python
import math
import jax, jax.numpy as jnp
from jax import lax
import numpy as np

D_MODEL = 1024
BATCH = 8
SEQ = 8192
DEPTH = 1

CHUNK = 64
D_CONV = D_MODEL
CONV_WIDTH = 31
SSM_EXPAND = 2
D_INNER = SSM_EXPAND * D_MODEL
HEAD_DIM = 64
N_SSM_HEADS = D_INNER // HEAD_DIM
D_STATE = 128
N_GROUPS = 4
SSM_CONV_WIDTH = 4
D_XBC = D_INNER + 2 * N_GROUPS * D_STATE
D_FF = ((8 * D_MODEL // 3 + 255) // 256) * 256
IN_COLS = 2 * D_CONV + D_INNER + D_XBC + N_SSM_HEADS + 2 * D_MODEL
NORM_EPS = 1e-6
LN_EPS = 1e-5

kernel_name = "gated_conformer_conv_mamba2_hybrid_block"


def rms_norm(x, w, eps=NORM_EPS):
    xf = x.astype(jnp.float32)
    y = xf * lax.rsqrt(jnp.mean(xf * xf, axis=-1, keepdims=True) + eps)
    return (y * w.astype(jnp.float32)).astype(x.dtype)


def layer_norm(x, w, b, eps=LN_EPS):
    xf = x.astype(jnp.float32)
    mu = jnp.mean(xf, axis=-1, keepdims=True)
    var = jnp.mean(jnp.square(xf - mu), axis=-1, keepdims=True)
    y = (xf - mu) * lax.rsqrt(var + eps)
    return (y * w.astype(jnp.float32) + b.astype(jnp.float32)).astype(x.dtype)


def causal_depthwise_conv(u, w, b):
    k, ch = w.shape
    out = lax.conv_general_dilated(
        u, w[:, None, :].astype(u.dtype), window_strides=(1,), padding=[(k - 1, 0)],
        dimension_numbers=("NWC", "WIO", "NWC"), feature_group_count=ch)
    return out + b.astype(u.dtype)


def adaln(c, w, b):
    mod = jax.nn.silu(c) @ w + b
    shift, scale, gate = jnp.split(mod, 3, axis=-1)
    return shift[:, None, :], scale[:, None, :], gate[:, None, :]


def ssd_chunked_scan(xs, dt, a, bm, cm):
    b, L, H, P = xs.shape
    G, N = bm.shape[2], bm.shape[3]
    J = H // G
    nc = L // CHUNK

    def to_chunks(t):
        return jnp.moveaxis(t.reshape(b, nc, CHUNK, *t.shape[2:]), 1, 0)

    xdt = (xs * dt[..., None]).reshape(b, L, G, J, P)
    da = (dt * a).reshape(b, L, G, J)
    xc, ac, bc, cc = to_chunks(xdt), to_chunks(da), to_chunks(bm), to_chunks(cm)
    causal = jnp.tril(jnp.ones((CHUNK, CHUNK), dtype=bool))[None, :, :, None, None]

    def step(state, inp):
        xk, ak, bk, ck = inp
        acs = jnp.cumsum(ak, axis=1)
        seg = acs[:, :, None] - acs[:, None, :]
        decay = jnp.exp(jnp.where(causal, seg, -jnp.inf))
        cb = jnp.einsum("blgn,bsgn->blsg", ck, bk)
        y_diag = jnp.einsum("blsg,blsgj,bsgjp->blgjp", cb, decay, xk)
        y_off = jnp.einsum("blgn,bgjpn,blgj->blgjp", ck, state, jnp.exp(acs))
        last = acs[:, -1]
        w_in = jnp.exp(last[:, None] - acs)
        new_state = state * jnp.exp(last)[..., None, None] + jnp.einsum(
            "bsgn,bsgj,bsgjp->bgjpn", bk, w_in, xk)
        return new_state, y_diag + y_off

    init = jnp.zeros((b, G, J, P, N), jnp.float32)
    _, ys = lax.scan(step, init, (xc, ac, bc, cc))
    return jnp.moveaxis(ys, 0, 1).reshape(b, L, H, P)


def mixer_sublayer(h, w_in, conv_dw_w, conv_dw_b, conv_ln_w, conv_ln_b, w_conv_out,
                   ssm_conv_w, ssm_conv_b, dt_bias, a_log, d_skip, ssm_norm_w, w_ssm_out, w_out):
    b, L, _ = h.shape
    proj = h @ w_in
    splits = np.cumsum([2 * D_CONV, D_INNER, D_XBC, N_SSM_HEADS, D_MODEL]).tolist()
    conv_in, z, xbc, dt_raw, g_conv, g_ssm = jnp.split(proj, splits, axis=-1)

    a_half, b_half = jnp.split(conv_in, 2, axis=-1)
    u = a_half * jax.nn.sigmoid(b_half)
    u = causal_depthwise_conv(u, conv_dw_w, conv_dw_b)
    u = jax.nn.silu(layer_norm(u, conv_ln_w, conv_ln_b))
    y_conv = u @ w_conv_out

    xbc = jax.nn.silu(causal_depthwise_conv(xbc, ssm_conv_w, ssm_conv_b))
    xs, bm, cm = jnp.split(xbc, [D_INNER, D_INNER + N_GROUPS * D_STATE], axis=-1)
    xs_f = xs.astype(jnp.float32).reshape(b, L, N_SSM_HEADS, HEAD_DIM)
    bm_f = bm.astype(jnp.float32).reshape(b, L, N_GROUPS, D_STATE)
    cm_f = cm.astype(jnp.float32).reshape(b, L, N_GROUPS, D_STATE)
    dt = jax.nn.softplus(dt_raw.astype(jnp.float32) + dt_bias.astype(jnp.float32))
    a = -jnp.exp(a_log.astype(jnp.float32))
    y = ssd_chunked_scan(xs_f, dt, a, bm_f, cm_f) + d_skip.astype(jnp.float32)[:, None] * xs_f
    y = y.reshape(b, L, D_INNER) * jax.nn.silu(z.astype(jnp.float32))
    yg = y.reshape(b, L, N_GROUPS, D_INNER // N_GROUPS)
    yg = yg * lax.rsqrt(jnp.mean(yg * yg, axis=-1, keepdims=True) + NORM_EPS)
    y = (yg.reshape(b, L, D_INNER) * ssm_norm_w.astype(jnp.float32)).astype(h.dtype)
    y_ssm = y @ w_ssm_out

    merged = jax.nn.sigmoid(g_conv) * y_conv + jax.nn.sigmoid(g_ssm) * y_ssm
    return merged @ w_out


def ffn_sublayer(h, w_ffn_in, w_ffn_down):
    gate, up = jnp.split(h @ w_ffn_in, 2, axis=-1)
    return (jax.nn.silu(gate) * up) @ w_ffn_down


def setup_inputs(seed: int = 0) -> dict:
    key = jax.random.key(seed)
    ks = jax.random.split(key, 32)
    f32 = jnp.float32
    nrm = lambda k, shape, s: jax.random.normal(k, shape, f32) * s
    gain = lambda k, shape: 1.0 + 0.05 * jax.random.normal(k, shape, f32)
    Ld = DEPTH
    u_dt = jax.random.uniform(ks[13], (Ld, N_SSM_HEADS), f32)
    dt0 = jnp.exp(u_dt * (math.log(0.1) - math.log(0.001)) + math.log(0.001))
    dt_bias = dt0 + jnp.log(-jnp.expm1(-dt0))
    a_log = jnp.log(jax.random.uniform(ks[14], (Ld, N_SSM_HEADS), f32, 1.0, 16.0))
    return {
        "x": jax.random.normal(ks[0], (BATCH, SEQ, D_MODEL), f32),
        "c": jax.random.normal(ks[1], (BATCH, D_MODEL), f32),
        "w_ada_mix": nrm(ks[2], (Ld, D_MODEL, 3 * D_MODEL), D_MODEL ** -0.5),
        "b_ada_mix": nrm(ks[3], (Ld, 3 * D_MODEL), 0.01),
        "norm_mix_w": gain(ks[4], (Ld, D_MODEL)),
        "w_in": nrm(ks[5], (Ld, D_MODEL, IN_COLS), D_MODEL ** -0.5),
        "conv_dw_w": nrm(ks[6], (Ld, CONV_WIDTH, D_CONV), CONV_WIDTH ** -0.5),
        "conv_dw_b": nrm(ks[7], (Ld, D_CONV), 0.01),
        "conv_ln_w": gain(ks[8], (Ld, D_CONV)),
        "conv_ln_b": nrm(ks[9], (Ld, D_CONV), 0.01),
        "w_conv_out": nrm(ks[10], (Ld, D_CONV, D_MODEL), D_CONV ** -0.5),
        "ssm_conv_w": nrm(ks[11], (Ld, SSM_CONV_WIDTH, D_XBC), SSM_CONV_WIDTH ** -0.5),
        "ssm_conv_b": nrm(ks[12], (Ld, D_XBC), 0.01),
        "dt_bias": dt_bias,
        "a_log": a_log,
        "d_skip": gain(ks[15], (Ld, N_SSM_HEADS)),
        "ssm_norm_w": gain(ks[16], (Ld, D_INNER)),
        "w_ssm_out": nrm(ks[17], (Ld, D_INNER, D_MODEL), D_INNER ** -0.5),
        "w_out": nrm(ks[18], (Ld, D_MODEL, D_MODEL), D_MODEL ** -0.5),
        "w_ada_ffn": nrm(ks[19], (Ld, D_MODEL, 3 * D_MODEL), D_MODEL ** -0.5),
        "b_ada_ffn": nrm(ks[20], (Ld, 3 * D_MODEL), 0.01),
        "norm_ffn_w": gain(ks[21], (Ld, D_MODEL)),
        "w_ffn_in": nrm(ks[22], (Ld, D_MODEL, 2 * D_FF), D_MODEL ** -0.5),
        "w_ffn_down": nrm(ks[23], (Ld, D_FF, D_MODEL), D_FF ** -0.5),
        "final_norm_w": gain(ks[24], (D_MODEL,)),
    }


def reference(x, c, w_ada_mix, b_ada_mix, norm_mix_w, w_in, conv_dw_w, conv_dw_b, conv_ln_w,
              conv_ln_b, w_conv_out, ssm_conv_w, ssm_conv_b, dt_bias, a_log, d_skip, ssm_norm_w,
              w_ssm_out, w_out, w_ada_ffn, b_ada_ffn, norm_ffn_w, w_ffn_in, w_ffn_down,
              final_norm_w):
    for i in range(DEPTH):
        shift, scale, gate = adaln(c, w_ada_mix[i], b_ada_mix[i])
        h = rms_norm(x, norm_mix_w[i]) * (1.0 + scale) + shift
        x = x + gate * mixer_sublayer(
            h, w_in[i], conv_dw_w[i], conv_dw_b[i], conv_ln_w[i], conv_ln_b[i], w_conv_out[i],
            ssm_conv_w[i], ssm_conv_b[i], dt_bias[i], a_log[i], d_skip[i], ssm_norm_w[i],
            w_ssm_out[i], w_out[i])
        shift, scale, gate = adaln(c, w_ada_ffn[i], b_ada_ffn[i])
        h = rms_norm(x, norm_ffn_w[i]) * (1.0 + scale) + shift
        x = x + gate * ffn_sublayer(h, w_ffn_in[i], w_ffn_down[i])
    return rms_norm(x, final_norm_w)
```

```python
import functools

import jax
import jax.numpy as jnp
from jax import lax
from jax.experimental import pallas as pl
from jax.experimental.pallas import tpu as pltpu

F32 = jnp.float32
BF16 = jnp.bfloat16

N_GROUPS = 4
D_STATE = 128
HEAD_DIM = 64
NORM_EPS = 1e-6
LN_EPS = 1e-5

LANES = 128
SSD_CHUNK = LANES
HEADS_PER_BLOCK = LANES // HEAD_DIM
MASK_BIAS = -1e30
VMEM_LIMIT = 56 * 1024 * 1024


def _sigmoid(v):
    return jax.nn.sigmoid(v)


def _silu(v):
    return v * jax.nn.sigmoid(v)


def _softplus(v):
    return jnp.maximum(v, 0.0) + jnp.log1p(jnp.exp(-jnp.abs(v)))


def _dot(a, b):
    return jnp.dot(a, b, preferred_element_type=F32)


def _resident(shape):
    nd = len(shape)
    return pl.BlockSpec(shape, lambda *_: (0,) * nd, pipeline_mode=pl.Buffered(1))


def _params(n_axes):
    return pltpu.CompilerParams(
        dimension_semantics=("arbitrary",) * n_axes, vmem_limit_bytes=VMEM_LIMIT)


def _ada_kernel(c_ref, w_ref, b_ref, o_ref):
    c = c_ref[...]
    o_ref[...] = jnp.dot(_silu(c), w_ref[...], preferred_element_type=F32,
                         precision=lax.Precision.HIGHEST) + b_ref[...]


def _ada(c, w, b):
    bsz, d = c.shape
    n = w.shape[1]
    tn = d
    return pl.pallas_call(
        _ada_kernel,
        grid=(n // tn,),
        in_specs=[pl.BlockSpec((bsz, d), lambda j: (0, 0)),
                  pl.BlockSpec((d, tn), lambda j: (0, j)),
                  pl.BlockSpec((1, tn), lambda j: (0, j))],
        out_specs=pl.BlockSpec((bsz, tn), lambda j: (0, j)),
        out_shape=jax.ShapeDtypeStruct((bsz, n), F32),
        compiler_params=_params(1),
        name="ada",
    )(c, w, b.reshape(1, n))


def _modulated_rms(x, nw, scale, shift):
    ms = jnp.mean(x * x, axis=-1, keepdims=True)
    h = x * lax.rsqrt(ms + NORM_EPS) * nw
    return h * (1.0 + scale) + shift


def _inproj_kernel(x_ref, shift_ref, scale_ref, nw_ref, w_ref, wdt_ref, dtb_ref,
                   u_ref, z_ref, xbc_ref, dt_ref, gc_ref, gs_ref, h_ref, *,
                   d_conv, d_inner, d_xbc, col_tile):
    h = _modulated_rms(x_ref[...], nw_ref[...], scale_ref[...], shift_ref[...])
    h_ref[...] = h.astype(BF16)

    def proj(c0):
        return _dot(h_ref[...], w_ref[:, c0:c0 + col_tile])

    for j in range(d_conv // col_tile):
        c0 = j * col_tile
        a = proj(c0)
        b = proj(d_conv + c0)
        u_ref[:, c0:c0 + col_tile] = (a * _sigmoid(b)).astype(BF16)
    off = 2 * d_conv
    for j in range(d_inner // col_tile):
        c0 = j * col_tile
        z_ref[:, c0:c0 + col_tile] = _silu(proj(off + c0)).astype(BF16)
    off += d_inner
    for j in range(d_xbc // col_tile):
        c0 = j * col_tile
        xbc_ref[:, c0:c0 + col_tile] = proj(off + c0).astype(BF16)
    off += d_xbc
    d_model = gc_ref.shape[-1]
    for j in range(d_model // col_tile):
        c0 = j * col_tile
        gc_ref[:, c0:c0 + col_tile] = _sigmoid(proj(off + c0)).astype(BF16)
    off += d_model
    for j in range(d_model // col_tile):
        c0 = j * col_tile
        gs_ref[:, c0:c0 + col_tile] = _sigmoid(proj(off + c0)).astype(BF16)
    dt_raw = _dot(h_ref[...], wdt_ref[...]) + dtb_ref[...]
    dt_ref[...] = _softplus(dt_raw)


def _inproj(x, shift, scale, nw, w_main, w_dt, dt_bias, *, d_conv, d_inner, d_xbc, tm):
    bsz, seq, d = x.shape
    col_tile = 512
    kern = functools.partial(_inproj_kernel, d_conv=d_conv, d_inner=d_inner,
                             d_xbc=d_xbc, col_tile=col_tile)
    row = lambda w: pl.BlockSpec((None, tm, w), lambda b, i: (b, i, 0))
    vec = pl.BlockSpec((None, 1, d), lambda b, i: (b, 0, 0))
    outs = [(d_conv, BF16), (d_inner, BF16), (d_xbc, BF16), (LANES, F32), (d, BF16), (d, BF16)]
    return pl.pallas_call(
        kern,
        grid=(bsz, seq // tm),
        in_specs=[row(d), vec, vec, _resident((1, d)), _resident(w_main.shape),
                  _resident(w_dt.shape), _resident((1, LANES))],
        out_specs=[row(w) for w, _ in outs],
        out_shape=[jax.ShapeDtypeStruct((bsz, seq, w), t) for w, t in outs],
        scratch_shapes=[pltpu.VMEM((tm, d), BF16)],
        compiler_params=_params(2),
        name="inproj",
    )(x, shift, scale, nw, w_main, w_dt, dt_bias)


def _conv_kernel(u_ref, gc_ref, cw_ref, cb_ref, lnw_ref, lnb_ref, wco_ref, o_ref,
                 ext_ref, v_ref, *, halo, row_tile):
    tl, d = u_ref.shape
    taps = cw_ref.shape[0]

    @pl.when(pl.program_id(1) == 0)
    def _():
        ext_ref[0:halo, :] = jnp.zeros((halo, d), F32)

    ext_ref[halo:halo + tl, :] = u_ref[...].astype(F32)

    first = halo - (taps - 1)
    for cb in range(d // LANES):
        cs = slice(cb * LANES, (cb + 1) * LANES)
        for rb in range(tl // row_tile):
            r0 = rb * row_tile
            acc = jnp.zeros((row_tile, LANES), F32) + cb_ref[:, cs]
            for k in range(taps):
                acc = acc + cw_ref[k:k + 1, cs] * ext_ref[r0 + first + k:r0 + first + k + row_tile, cs]
            v_ref[r0:r0 + row_tile, cs] = acc

    ext_ref[0:halo, :] = ext_ref[tl:tl + halo, :]

    v = v_ref[...]
    mu = jnp.mean(v, axis=-1, keepdims=True)
    cen = v - mu
    var = jnp.mean(cen * cen, axis=-1, keepdims=True)
    y = cen * lax.rsqrt(var + LN_EPS) * lnw_ref[...] + lnb_ref[...]
    yc = _dot(_silu(y).astype(BF16), wco_ref[...])
    o_ref[...] = (yc * gc_ref[...].astype(F32)).astype(BF16)


def _conv_branch(u, gc, cw, cb, lnw, lnb, wco, *, tl):
    bsz, seq, d = u.shape
    taps = cw.shape[0]
    halo = -(-(taps - 1) // 8) * 8
    kern = functools.partial(_conv_kernel, halo=halo, row_tile=64)
    row = pl.BlockSpec((None, tl, d), lambda b, i: (b, i, 0))
    return pl.pallas_call(
        kern,
        grid=(bsz, seq // tl),
        in_specs=[row, row, _resident(cw.shape), _resident((1, d)), _resident((1, d)),
                  _resident((1, d)), _resident(wco.shape)],
        out_specs=row,
        out_shape=jax.ShapeDtypeStruct((bsz, seq, d), BF16),
        scratch_shapes=[pltpu.VMEM((tl + halo, d), F32), pltpu.VMEM((tl, d), F32)],
        compiler_params=_params(2),
        name="convbranch",
    )(u, gc, cw, cb, lnw, lnb, wco)


def _split3(v):
    hi = v.astype(BF16)
    r1 = v - hi.astype(F32)
    mid = r1.astype(BF16)
    lo = (r1 - mid.astype(F32)).astype(BF16)
    return hi, mid, lo


def _ssm_kernel(xbc_ref, z_ref, dt_ref, gs_ref, ycg_ref, x_ref, gate_ref,
                scw_ref, scb_ref, alog_ref, dskip_ref, snw_ref, wso_ref, wout_ref,
                o_ref,
                ext_ref, xs_ref, bm_ref, cm_ref, st_ref, y_ref, *, halo):
    tl = xbc_ref.shape[0]
    d_inner = z_ref.shape[1]
    taps = scw_ref.shape[0]
    n_blocks = d_inner // LANES
    blocks_per_group = n_blocks // N_GROUPS
    ct = SSD_CHUNK

    @pl.when(pl.program_id(1) == 0)
    def _():
        ext_ref[0:halo, :] = jnp.zeros((halo, ext_ref.shape[1]), F32)
        st_ref[...] = jnp.zeros(st_ref.shape, F32)

    ext_ref[halo:halo + tl, :] = xbc_ref[...].astype(F32)
    first = halo - (taps - 1)
    for cb in range(ext_ref.shape[1] // LANES):
        cs = slice(cb * LANES, (cb + 1) * LANES)
        acc = jnp.zeros((tl, LANES), F32) + scb_ref[:, cs]
        for k in range(taps):
            acc = acc + scw_ref[k:k + 1, cs] * ext_ref[first + k:first + k + tl, cs]
        act = _silu(acc)
        if cb < n_blocks:
            xs_ref[cb] = act
        elif cb < n_blocks + N_GROUPS:
            bm_ref[cb - n_blocks] = act
        else:
            cm_ref[cb - n_blocks - N_GROUPS] = act
    ext_ref[0:halo, :] = ext_ref[tl:tl + halo, :]

    li = lax.broadcasted_iota(jnp.int32, (ct, ct), 0)
    si = lax.broadcasted_iota(jnp.int32, (ct, ct), 1)
    causal = li >= si
    tri = jnp.where(causal, 1.0, 0.0).astype(BF16)
    mask_bias = jnp.where(causal, 0.0, MASK_BIAS)
    low_half = si < HEAD_DIM

    def chunk(c, carry):
        r0 = pl.multiple_of(c * ct, ct)
        rows = pl.ds(r0, ct)
        dtc = dt_ref[rows, :]
        da = dtc * -jnp.exp(alog_ref[...])
        hi, mid, lo = _split3(da)
        acs = _dot(tri, hi) + _dot(tri, mid) + _dot(tri, lo)
        acs_t = acs.T
        dt_t = dtc.T
        last_t = acs_t[:, ct - 1:ct]
        w_t = dt_t * jnp.exp(last_t - acs_t)
        e_last = jnp.exp(acs[ct - 1:ct, :])
        for g in range(N_GROUPS):
            cg = cm_ref[g, rows, :]
            bg = bm_ref[g, rows, :]
            cg_b = cg.astype(BF16)
            cb_mat = lax.dot_general(cg_b, bg.astype(BF16), (((1,), (1,)), ((), ())),
                                     preferred_element_type=F32)
            bg_t = bg.T
            st_g = st_ref[g]
            y_off = _dot(cg_b, st_g.astype(BF16))
            new_cols = []
            for q in range(blocks_per_group):
                blk = g * blocks_per_group + q
                h0 = blk * HEADS_PER_BLOCK
                xblk = xs_ref[blk, rows, :]
                m_parts, b_parts, e_cols = [], [], []
                for hh in (h0, h0 + 1):
                    colb = jnp.broadcast_to(acs[:, hh:hh + 1], (ct, ct))
                    m_parts.append(jnp.exp(colb - acs_t[hh:hh + 1, :] + mask_bias)
                                   * cb_mat * dt_t[hh:hh + 1, :])
                    b_parts.append(bg_t * w_t[hh:hh + 1, :])
                    e_cols.append(jnp.exp(colb))
                x_lo = jnp.where(low_half, xblk, 0.0)
                x_hi = xblk - x_lo
                x_bd = jnp.concatenate([x_lo, x_hi], axis=0).astype(BF16)
                m_cat = jnp.concatenate(m_parts, axis=1).astype(BF16)
                y_diag = _dot(m_cat, x_bd)
                e_blk = jnp.where(low_half, e_cols[0], e_cols[1])
                cs = slice(q * LANES, (q + 1) * LANES)
                y_blk = y_diag + e_blk * y_off[:, cs]
                y_ref[rows, blk * LANES:(blk + 1) * LANES] = y_blk + dskip_ref[:, blk * LANES:(blk + 1) * LANES] * xblk
                b_cat = jnp.concatenate(b_parts, axis=1).astype(BF16)
                dec = jnp.where(low_half[0:1, :], e_last[:, h0:h0 + 1], e_last[:, h0 + 1:h0 + 2])
                new_cols.append(st_g[:, cs] * dec + _dot(b_cat, x_bd))
            st_ref[g] = jnp.concatenate(new_cols, axis=1)
        return carry

    lax.fori_loop(0, tl // ct, chunk, 0)

    gw = d_inner // N_GROUPS
    parts = []
    for g in range(N_GROUPS):
        gsl = slice(g * gw, (g + 1) * gw)
        yg = y_ref[:, gsl] * z_ref[:, gsl].astype(F32)
        ms = jnp.mean(yg * yg, axis=-1, keepdims=True)
        parts.append((yg * lax.rsqrt(ms + NORM_EPS) * snw_ref[:, gsl]).astype(BF16))
    yn = jnp.concatenate(parts, axis=1)
    y_ssm = _dot(yn, wso_ref[...])
    merged = ycg_ref[...].astype(F32) + gs_ref[...].astype(F32) * y_ssm
    mix = _dot(merged.astype(BF16), wout_ref[...])
    o_ref[...] = x_ref[...] + gate_ref[...] * mix


def _ssm_branch(xbc, z, dt, gs, ycg, x, gate, scw, scb, alog_pad, dskip, snw, wso, wout, *, tl):
    bsz, seq, d = x.shape
    d_xbc = xbc.shape[-1]
    d_inner = z.shape[-1]
    halo = 8
    n_blocks = d_inner // LANES
    kern = functools.partial(_ssm_kernel, halo=halo)
    row = lambda w: pl.BlockSpec((None, tl, w), lambda b, i: (b, i, 0))
    vec = pl.BlockSpec((None, 1, d), lambda b, i: (b, 0, 0))
    return pl.pallas_call(
        kern,
        grid=(bsz, seq // tl),
        in_specs=[row(d_xbc), row(d_inner), row(LANES), row(d), row(d), row(d), vec,
                  _resident(scw.shape), _resident((1, d_xbc)), _resident((1, LANES)),
                  _resident((1, d_inner)), _resident((1, d_inner)),
                  _resident(wso.shape), _resident(wout.shape)],
        out_specs=row(d),
        out_shape=jax.ShapeDtypeStruct((bsz, seq, d), F32),
        scratch_shapes=[pltpu.VMEM((tl + halo, d_xbc), F32),
                        pltpu.VMEM((n_blocks, tl, LANES), F32),
                        pltpu.VMEM((N_GROUPS, tl, D_STATE), F32),
                        pltpu.VMEM((N_GROUPS, tl, D_STATE), F32),
                        pltpu.VMEM((N_GROUPS, D_STATE, d_inner // N_GROUPS), F32),
                        pltpu.VMEM((tl, d_inner), F32)],
        compiler_params=_params(2),
        name="ssmbranch",
    )(xbc, z, dt, gs, ycg, x, gate, scw, scb, alog_pad, dskip, snw, wso, wout)


def _ffn_kernel(x_ref, shift_ref, scale_ref, gate_ref, nw_ref, fnw_ref, win_ref, wdn_ref,
                o_ref, h_ref, *, slabs):
    x = x_ref[...]
    h = _modulated_rms(x, nw_ref[...], scale_ref[...], shift_ref[...])
    h_ref[...] = h.astype(BF16)
    d_ff = wdn_ref.shape[0]
    acc = jnp.zeros(x.shape, F32)
    for s0, s1 in slabs:
        g = _dot(h_ref[...], win_ref[:, s0:s1])
        u = _dot(h_ref[...], win_ref[:, d_ff + s0:d_ff + s1])
        acc = acc + _dot((_silu(g) * u).astype(BF16), wdn_ref[s0:s1, :])
    x2 = x + gate_ref[...] * acc
    ms = jnp.mean(x2 * x2, axis=-1, keepdims=True)
    o_ref[...] = x2 * lax.rsqrt(ms + NORM_EPS) * fnw_ref[...]


def _ffn(x, shift, scale, gate, nw, fnw, win, wdn, *, tm):
    bsz, seq, d = x.shape
    d_ff = wdn.shape[0]
    mxu_cols = 256
    half = (d_ff // mxu_cols + 1) // 2 * mxu_cols
    kern = functools.partial(_ffn_kernel, slabs=((0, half), (half, d_ff)))
    row = pl.BlockSpec((None, tm, d), lambda b, i: (b, i, 0))
    vec = pl.BlockSpec((None, 1, d), lambda b, i: (b, 0, 0))
    return pl.pallas_call(
        kern,
        grid=(bsz, seq // tm),
        in_specs=[row, vec, vec, vec, _resident((1, d)), _resident((1, d)),
                  _resident(win.shape), _resident(wdn.shape)],
        out_specs=row,
        out_shape=jax.ShapeDtypeStruct((bsz, seq, d), F32),
        scratch_shapes=[pltpu.VMEM((tm, d), BF16)],
        compiler_params=_params(2),
        name="ffn",
    )(x, shift, scale, gate, nw, fnw, win, wdn)


def _mods(c, w, b):
    bsz, d = c.shape
    mod = _ada(c, w, b)
    return [mod[:, i * d:(i + 1) * d].reshape(bsz, 1, d) for i in range(3)]


def _pad_lanes(v, width=LANES):
    return jnp.pad(v, [(0, 0)] * (v.ndim - 1) + [(0, width - v.shape[-1])])


def kernel(x, c, w_ada_mix, b_ada_mix, norm_mix_w, w_in, conv_dw_w, conv_dw_b, conv_ln_w, conv_ln_b, w_conv_out, ssm_conv_w, ssm_conv_b, dt_bias, a_log, d_skip, ssm_norm_w, w_ssm_out, w_out, w_ada_ffn, b_ada_ffn, norm_ffn_w, w_ffn_in, w_ffn_down, final_norm_w):
    bsz, seq, d = x.shape
    depth = w_in.shape[0]
    assert depth == 1, "the FFN kernel applies the final RMSNorm, so one layer only"
    d_conv = conv_dw_w.shape[-1]
    d_inner = ssm_norm_w.shape[-1]
    d_xbc = ssm_conv_w.shape[-1]
    n_heads = dt_bias.shape[-1]
    assert d_inner == n_heads * HEAD_DIM and d_xbc == d_inner + 2 * N_GROUPS * D_STATE
    assert n_heads <= LANES and seq % 512 == 0
    row_vec = lambda v: v.reshape(1, -1).astype(F32)

    for i in range(depth):
        shift, scale, gate = _mods(c, w_ada_mix[i], b_ada_mix[i])
        w = w_in[i]
        s_dt = 2 * d_conv + d_inner + d_xbc
        w_main = jnp.concatenate([w[:, :s_dt], w[:, s_dt + n_heads:]], axis=1).astype(BF16)
        w_dt = _pad_lanes(w[:, s_dt:s_dt + n_heads]).astype(BF16)
        u, z, xbc, dt, gc, gs = _inproj(
            x, shift, scale, row_vec(norm_mix_w[i]), w_main, w_dt,
            _pad_lanes(row_vec(dt_bias[i])), d_conv=d_conv, d_inner=d_inner, d_xbc=d_xbc, tm=512)
        ycg = _conv_branch(u, gc, conv_dw_w[i], row_vec(conv_dw_b[i]), row_vec(conv_ln_w[i]),
                           row_vec(conv_ln_b[i]), w_conv_out[i].astype(BF16), tl=256)
        alog_pad = _pad_lanes(row_vec(a_log[i]))
        dskip = row_vec(jnp.repeat(d_skip[i].astype(F32), HEAD_DIM))
        x = _ssm_branch(xbc, z, dt, gs, ycg, x, gate, ssm_conv_w[i], row_vec(ssm_conv_b[i]),
                        alog_pad, dskip, row_vec(ssm_norm_w[i]), w_ssm_out[i].astype(BF16),
                        w_out[i].astype(BF16), tl=256)
        shift, scale, gate = _mods(c, w_ada_ffn[i], b_ada_ffn[i])
        x = _ffn(x, shift, scale, gate, row_vec(norm_ffn_w[i]), row_vec(final_norm_w),
                 w_ffn_in[i].astype(BF16), w_ffn_down[i].astype(BF16), tm=512)
    return x
```

```python
import functools
import math

import jax
import jax.numpy as jnp
from jax import lax
from jax.experimental import pallas as pl
from jax.experimental.pallas import tpu as pltpu

F32 = jnp.float32
BF16 = jnp.bfloat16

N_GROUPS = 4
D_STATE = 128
HEAD_DIM = 64
NORM_EPS = 1e-6
LN_EPS = 1e-5

LANES = 128
SUBLANES = 8
MXU_COLS = 256
SSD_CHUNK = LANES
HEADS_PER_BLOCK = LANES // HEAD_DIM
MASK_BIAS = -1e30
LOG2E = math.log2(math.e)
VMEM_LIMIT = 56 * 1024 * 1024

INPROJ_ROWS = 512
SSD_ROWS = 512
MIXFFN_ROWS = 256
COL_TILE = 512
CONV_ROW_TILE = 64


def _sigmoid(v):
    return jax.nn.sigmoid(v)


def _silu(v):
    return v * jax.nn.sigmoid(v)


def _softplus(v):
    return jnp.maximum(v, 0.0) + jnp.log1p(jnp.exp(-jnp.abs(v)))


def _dot(a, b):
    return jnp.dot(a, b, preferred_element_type=F32)


def _resident(shape):
    nd = len(shape)
    return pl.BlockSpec(shape, lambda *_: (0,) * nd, pipeline_mode=pl.Buffered(1))


def _params(n_axes):
    return pltpu.CompilerParams(
        dimension_semantics=("arbitrary",) * n_axes, vmem_limit_bytes=VMEM_LIMIT)


def _round_up(n, m):
    return -(-n // m) * m


def _ada_kernel(c_ref, w_ref, b_ref, o_ref):
    c = c_ref[...]
    o_ref[...] = jnp.dot(_silu(c), w_ref[...], preferred_element_type=F32,
                         precision=lax.Precision.HIGHEST) + b_ref[...]


def _ada(c, w, b):
    bsz, d = c.shape
    n = w.shape[1]
    tn = d
    return pl.pallas_call(
        _ada_kernel,
        grid=(n // tn,),
        in_specs=[pl.BlockSpec((bsz, d), lambda j: (0, 0)),
                  pl.BlockSpec((d, tn), lambda j: (0, j)),
                  pl.BlockSpec((1, tn), lambda j: (0, j))],
        out_specs=pl.BlockSpec((bsz, tn), lambda j: (0, j)),
        out_shape=jax.ShapeDtypeStruct((bsz, n), F32),
        compiler_params=_params(1),
        name="ada",
    )(c, w, b.reshape(1, n))


def _modulated_rms(x, nw, scale, shift):
    ms = jnp.mean(x * x, axis=-1, keepdims=True)
    h = x * lax.rsqrt(ms + NORM_EPS) * nw
    return h * (1.0 + scale) + shift


def _inproj_kernel(x_ref, shift_ref, scale_ref, nw_ref, w_ref, wdt_ref, dtb_ref,
                   scw_ref, scb_ref,
                   u_ref, z_ref, xbc_ref, dt_ref, gc_ref, gs_ref,
                   h_ref, work_ref, halo_ref, *, d_conv, d_inner, d_xbc):
    tm = x_ref.shape[0]
    taps = scw_ref.shape[0]
    halo = halo_ref.shape[0]

    @pl.when(pl.program_id(1) == 0)
    def _():
        halo_ref[...] = jnp.zeros(halo_ref.shape, F32)

    h = _modulated_rms(x_ref[...], nw_ref[...], scale_ref[...], shift_ref[...])
    h_ref[...] = h.astype(BF16)

    def proj(c0):
        return _dot(h_ref[...], w_ref[:, c0:c0 + COL_TILE])

    d_model = gc_ref.shape[-1]
    z_off = 2 * d_conv
    xbc_off = z_off + d_inner
    gc_off = xbc_off + d_xbc
    gs_off = gc_off + d_model
    first = halo - (taps - 1)

    def glu_task(c0):
        a = proj(c0)
        b = proj(d_conv + c0)
        u_ref[:, c0:c0 + COL_TILE] = (a * _sigmoid(b)).astype(BF16)

    def act_task(out_ref, act, off, c0):
        out_ref[:, c0:c0 + COL_TILE] = act(proj(off + c0)).astype(BF16)

    def xbc_task(j):
        c0 = j * COL_TILE
        cs = slice(c0, c0 + COL_TILE)
        work = work_ref.at[j % 2]
        work[0:halo, :] = halo_ref[:, cs]
        work[halo:halo + tm, :] = proj(xbc_off + c0)
        acc = jnp.zeros((tm, COL_TILE), F32) + scb_ref[:, cs]
        for k in range(taps):
            acc = acc + scw_ref[k:k + 1, cs] * work[first + k:first + k + tm, :]
        xbc_ref[:, cs] = _silu(acc).astype(BF16)
        halo_ref[:, cs] = work[tm:tm + halo, :]

    light = [functools.partial(glu_task, j * COL_TILE) for j in range(d_conv // COL_TILE)]
    light += [functools.partial(act_task, z_ref, _silu, z_off, j * COL_TILE)
              for j in range(d_inner // COL_TILE)]
    light += [functools.partial(act_task, gc_ref, _sigmoid, gc_off, j * COL_TILE)
              for j in range(d_model // COL_TILE)]
    light += [functools.partial(act_task, gs_ref, _sigmoid, gs_off, j * COL_TILE)
              for j in range(d_model // COL_TILE)]
    heavy = [functools.partial(xbc_task, j) for j in range(d_xbc // COL_TILE)]
    per_heavy = -(-len(light) // len(heavy))
    for j, task in enumerate(heavy):
        task()
        for t in light[j * per_heavy:(j + 1) * per_heavy]:
            t()
    dt_raw = _dot(h_ref[...], wdt_ref[...]) + dtb_ref[...]
    dt_ref[...] = _softplus(dt_raw)


def _inproj(x, shift, scale, nw, w_main, w_dt, dt_bias, scw, scb, *, d_conv, d_inner, d_xbc):
    bsz, seq, d = x.shape
    tm = INPROJ_ROWS
    halo = _round_up(scw.shape[0] - 1, SUBLANES)
    kern = functools.partial(_inproj_kernel, d_conv=d_conv, d_inner=d_inner, d_xbc=d_xbc)
    row = lambda w: pl.BlockSpec((None, tm, w), lambda b, i: (b, i, 0))
    vec = pl.BlockSpec((None, 1, d), lambda b, i: (b, 0, 0))
    outs = [(d_conv, BF16), (d_inner, BF16), (d_xbc, BF16), (LANES, F32), (d, BF16), (d, BF16)]
    return pl.pallas_call(
        kern,
        grid=(bsz, seq // tm),
        in_specs=[row(d), vec, vec, _resident((1, d)), _resident(w_main.shape),
                  _resident(w_dt.shape), _resident((1, LANES)),
                  _resident(scw.shape), _resident((1, d_xbc))],
        out_specs=[row(w) for w, _ in outs],
        out_shape=[jax.ShapeDtypeStruct((bsz, seq, w), t) for w, t in outs],
        scratch_shapes=[pltpu.VMEM((tm, d), BF16),
                        pltpu.VMEM((2, tm + halo, COL_TILE), F32),
                        pltpu.VMEM((halo, d_xbc), F32)],
        compiler_params=_params(2),
        name="inproj",
    )(x, shift, scale, nw, w_main, w_dt, dt_bias, scw, scb)


def _split3(v):
    hi = v.astype(BF16)
    r1 = v - hi.astype(F32)
    mid = r1.astype(BF16)
    lo = (r1 - mid.astype(F32)).astype(BF16)
    return hi, mid, lo


def _ssd_kernel(xbc_ref, z_ref, dt_ref, gs_ref, alog_ref, dskip_ref, snw_ref, wso_ref,
                o_ref, st_ref, y_ref):
    tl = xbc_ref.shape[0]
    d_inner = z_ref.shape[1]
    n_blocks = d_inner // LANES
    blocks_per_group = n_blocks // N_GROUPS
    b_off = d_inner
    c_off = d_inner + N_GROUPS * D_STATE
    ct = SSD_CHUNK

    @pl.when(pl.program_id(1) == 0)
    def _():
        st_ref[...] = jnp.zeros(st_ref.shape, F32)

    li = lax.broadcasted_iota(jnp.int32, (ct, ct), 0)
    si = lax.broadcasted_iota(jnp.int32, (ct, ct), 1)
    causal = li >= si
    tri = jnp.where(causal, 1.0, 0.0).astype(BF16)
    mask_bias = jnp.where(causal, 0.0, MASK_BIAS)
    low_half = si < HEAD_DIM

    def chunk(c, carry):
        r0 = pl.multiple_of(c * ct, ct)
        rows = pl.ds(r0, ct)
        dtc = dt_ref[rows, :]
        da = dtc * -jnp.exp(alog_ref[...])
        hi, mid, lo = _split3(da)
        acs = _dot(tri, hi) + _dot(tri, mid) + _dot(tri, lo)
        acs2 = acs * LOG2E
        acs2_t = acs2.T
        dt_t = dtc.T
        row2_t = acs2_t - jnp.log2(dt_t)
        w_t = dt_t * jnp.exp2(acs2_t[:, ct - 1:ct] - acs2_t)
        e_last = jnp.exp2(acs2[ct - 1:ct, :])
        for g in range(N_GROUPS):
            cg_b = xbc_ref[rows, c_off + g * D_STATE:c_off + (g + 1) * D_STATE]
            bg_b = xbc_ref[rows, b_off + g * D_STATE:b_off + (g + 1) * D_STATE]
            cb_mat = lax.dot_general(cg_b, bg_b, (((1,), (1,)), ((), ())),
                                     preferred_element_type=F32)
            bg_t = bg_b.astype(F32).T
            st_g = st_ref[g]
            y_off = _dot(cg_b, st_g.astype(BF16))
            new_cols = []
            for q in range(blocks_per_group):
                blk = g * blocks_per_group + q
                h0 = blk * HEADS_PER_BLOCK
                bs = slice(blk * LANES, (blk + 1) * LANES)
                xblk = xbc_ref[rows, bs]
                m_parts, b_parts, e_cols = [], [], []
                for hh in (h0, h0 + 1):
                    colb = jnp.broadcast_to(acs2[:, hh:hh + 1], (ct, ct))
                    m_parts.append(jnp.exp2(colb - row2_t[hh:hh + 1, :] + mask_bias) * cb_mat)
                    b_parts.append(bg_t * w_t[hh:hh + 1, :])
                    e_cols.append(jnp.exp2(colb))
                x_lo = jnp.where(low_half, xblk, jnp.zeros_like(xblk))
                x_hi = jnp.where(low_half, jnp.zeros_like(xblk), xblk)
                x_bd = jnp.concatenate([x_lo, x_hi], axis=0)
                m_cat = jnp.concatenate(m_parts, axis=1).astype(BF16)
                y_diag = _dot(m_cat, x_bd)
                e_blk = jnp.where(low_half, e_cols[0], e_cols[1])
                cs = slice(q * LANES, (q + 1) * LANES)
                y_ref[rows, bs] = (y_diag + e_blk * y_off[:, cs]
                                   + dskip_ref[:, bs] * xblk.astype(F32))
                b_cat = jnp.concatenate(b_parts, axis=1).astype(BF16)
                dec = jnp.where(low_half[0:1, :], e_last[:, h0:h0 + 1], e_last[:, h0 + 1:h0 + 2])
                new_cols.append(st_g[:, cs] * dec + _dot(b_cat, x_bd))
            st_ref[g] = jnp.concatenate(new_cols, axis=1)
        return carry

    lax.fori_loop(0, tl // ct, chunk, 0)

    gw = d_inner // N_GROUPS
    parts = []
    for g in range(N_GROUPS):
        gsl = slice(g * gw, (g + 1) * gw)
        yg = y_ref[:, gsl] * z_ref[:, gsl].astype(F32)
        ms = jnp.mean(yg * yg, axis=-1, keepdims=True)
        parts.append((yg * lax.rsqrt(ms + NORM_EPS) * snw_ref[:, gsl]).astype(BF16))
    yn = jnp.concatenate(parts, axis=1)
    o_ref[...] = (gs_ref[...].astype(F32) * _dot(yn, wso_ref[...])).astype(BF16)


def _ssd_branch(xbc, z, dt, gs, alog_pad, dskip, snw, wso):
    bsz, seq, d = gs.shape
    d_xbc = xbc.shape[-1]
    d_inner = z.shape[-1]
    tl = SSD_ROWS
    row = lambda w: pl.BlockSpec((None, tl, w), lambda b, i: (b, i, 0))
    return pl.pallas_call(
        _ssd_kernel,
        grid=(bsz, seq // tl),
        in_specs=[row(d_xbc), row(d_inner), row(LANES), row(d),
                  _resident((1, LANES)), _resident((1, d_inner)), _resident((1, d_inner)),
                  _resident(wso.shape)],
        out_specs=row(d),
        out_shape=jax.ShapeDtypeStruct((bsz, seq, d), BF16),
        scratch_shapes=[pltpu.VMEM((N_GROUPS, D_STATE, d_inner // N_GROUPS), F32),
                        pltpu.VMEM((tl, d_inner), F32)],
        compiler_params=_params(2),
        name="ssd",
    )(xbc, z, dt, gs, alog_pad, dskip, snw, wso)


def _mixffn_kernel(u_ref, gc_ref, ysg_ref, x_ref, gate1_ref,
                   cw_ref, cb_ref, lnw_ref, lnb_ref, wco_ref, wout_ref,
                   shift_ref, scale_ref, gate2_ref, nw_ref, fnw_ref, win_ref, wdn_ref,
                   o_ref,
                   ext_ref, ph_ref, v_ref, h_ref, *, halo, slabs):
    tl, d = u_ref.shape
    taps = cw_ref.shape[0]
    ph_rows = ph_ref.shape[2]

    @pl.when(pl.program_id(1) == 0)
    def _():
        ext_ref[0:halo, :] = jnp.zeros((halo, d), F32)

    ext_ref[halo:halo + tl, :] = u_ref[...].astype(F32)

    first = halo - (taps - 1)
    for cb in range(d // LANES):
        cs = slice(cb * LANES, (cb + 1) * LANES)
        buf = cb % 2
        for p in range(1, SUBLANES):
            ph_ref[buf, p - 1] = ext_ref[p:p + ph_rows, cs]
        for rb in range(tl // CONV_ROW_TILE):
            r0 = rb * CONV_ROW_TILE
            acc = jnp.zeros((CONV_ROW_TILE, LANES), F32) + cb_ref[:, cs]
            for k in range(taps):
                a8, p = divmod(first + k, SUBLANES)
                lo = r0 + a8 * SUBLANES
                if p == 0:
                    src = ext_ref[lo:lo + CONV_ROW_TILE, cs]
                else:
                    src = ph_ref[buf, p - 1, lo:lo + CONV_ROW_TILE, :]
                acc = acc + cw_ref[k:k + 1, cs] * src
            v_ref[r0:r0 + CONV_ROW_TILE, cs] = acc

    ext_ref[0:halo, :] = ext_ref[tl:tl + halo, :]

    v = v_ref[...]
    mu = jnp.mean(v, axis=-1, keepdims=True)
    cen = v - mu
    var = jnp.mean(cen * cen, axis=-1, keepdims=True)
    y = cen * lax.rsqrt(var + LN_EPS) * lnw_ref[...] + lnb_ref[...]
    y_conv = _dot(_silu(y).astype(BF16), wco_ref[...])
    merged = gc_ref[...].astype(F32) * y_conv + ysg_ref[...].astype(F32)
    x1 = x_ref[...] + gate1_ref[...] * _dot(merged.astype(BF16), wout_ref[...])

    h = _modulated_rms(x1, nw_ref[...], scale_ref[...], shift_ref[...])
    h_ref[...] = h.astype(BF16)
    d_ff = wdn_ref.shape[0]
    acc = jnp.zeros((tl, d), F32)
    for s0, s1 in slabs:
        g = _dot(h_ref[...], win_ref[:, s0:s1])
        up = _dot(h_ref[...], win_ref[:, d_ff + s0:d_ff + s1])
        acc = acc + _dot((_silu(g) * up).astype(BF16), wdn_ref[s0:s1, :])
    x2 = x1 + gate2_ref[...] * acc
    ms = jnp.mean(x2 * x2, axis=-1, keepdims=True)
    o_ref[...] = x2 * lax.rsqrt(ms + NORM_EPS) * fnw_ref[...]


def _mixffn(u, gc, ysg, x, gate1, cw, cb, lnw, lnb, wco, wout,
            shift, scale, gate2, nw, fnw, win, wdn):
    bsz, seq, d = x.shape
    tl = MIXFFN_ROWS
    taps = cw.shape[0]
    halo = _round_up(taps - 1, SUBLANES)
    first = halo - (taps - 1)
    ph_rows = tl + max((first + k) // SUBLANES * SUBLANES
                       for k in range(taps) if (first + k) % SUBLANES)
    d_ff = wdn.shape[0]
    half = (d_ff // MXU_COLS + 1) // 2 * MXU_COLS
    kern = functools.partial(_mixffn_kernel, halo=halo, slabs=((0, half), (half, d_ff)))
    row = pl.BlockSpec((None, tl, d), lambda b, i: (b, i, 0))
    vec = pl.BlockSpec((None, 1, d), lambda b, i: (b, 0, 0))
    r1 = _resident((1, d))
    return pl.pallas_call(
        kern,
        grid=(bsz, seq // tl),
        in_specs=[row, row, row, row, vec,
                  _resident(cw.shape), r1, r1, r1, _resident(wco.shape), _resident(wout.shape),
                  vec, vec, vec, r1, r1, _resident(win.shape), _resident(wdn.shape)],
        out_specs=row,
        out_shape=jax.ShapeDtypeStruct((bsz, seq, d), F32),
        scratch_shapes=[pltpu.VMEM((tl + halo, d), F32),
                        pltpu.VMEM((2, SUBLANES - 1, ph_rows, LANES), F32),
                        pltpu.VMEM((tl, d), F32),
                        pltpu.VMEM((tl, d), BF16)],
        compiler_params=_params(2),
        name="mixffn",
    )(u, gc, ysg, x, gate1, cw, cb, lnw, lnb, wco, wout, shift, scale, gate2, nw, fnw, win, wdn)


def _mods(c, w, b):
    bsz, d = c.shape
    mod = _ada(c, w, b)
    return [mod[:, i * d:(i + 1) * d].reshape(bsz, 1, d) for i in range(3)]


def _pad_lanes(v, width=LANES):
    return jnp.pad(v, [(0, 0)] * (v.ndim - 1) + [(0, width - v.shape[-1])])


def kernel(x, c, w_ada_mix, b_ada_mix, norm_mix_w, w_in, conv_dw_w, conv_dw_b, conv_ln_w, conv_ln_b, w_conv_out, ssm_conv_w, ssm_conv_b, dt_bias, a_log, d_skip, ssm_norm_w, w_ssm_out, w_out, w_ada_ffn, b_ada_ffn, norm_ffn_w, w_ffn_in, w_ffn_down, final_norm_w):
    bsz, seq, d = x.shape
    depth = w_in.shape[0]
    assert depth == 1, "the last kernel applies the final RMSNorm, so one layer only"
    d_conv = conv_dw_w.shape[-1]
    d_inner = ssm_norm_w.shape[-1]
    d_xbc = ssm_conv_w.shape[-1]
    n_heads = dt_bias.shape[-1]
    assert d_inner == n_heads * HEAD_DIM and d_xbc == d_inner + 2 * N_GROUPS * D_STATE
    assert n_heads <= LANES and seq % max(INPROJ_ROWS, SSD_ROWS, MIXFFN_ROWS) == 0
    row_vec = lambda v: v.reshape(1, -1).astype(F32)
    i = 0

    shift, scale, gate = _mods(c, w_ada_mix[i], b_ada_mix[i])
    w = w_in[i]
    s_dt = 2 * d_conv + d_inner + d_xbc
    w_main = jnp.concatenate([w[:, :s_dt], w[:, s_dt + n_heads:]], axis=1).astype(BF16)
    w_dt = _pad_lanes(w[:, s_dt:s_dt + n_heads]).astype(BF16)
    u, z, xbc, dt, gc, gs = _inproj(
        x, shift, scale, row_vec(norm_mix_w[i]), w_main, w_dt, _pad_lanes(row_vec(dt_bias[i])),
        ssm_conv_w[i], row_vec(ssm_conv_b[i]), d_conv=d_conv, d_inner=d_inner, d_xbc=d_xbc)
    dskip = row_vec(jnp.repeat(d_skip[i].astype(F32), HEAD_DIM))
    ysg = _ssd_branch(xbc, z, dt, gs, _pad_lanes(row_vec(a_log[i])), dskip,
                      row_vec(ssm_norm_w[i]), w_ssm_out[i].astype(BF16))
    shift2, scale2, gate2 = _mods(c, w_ada_ffn[i], b_ada_ffn[i])
    return _mixffn(u, gc, ysg, x, gate, conv_dw_w[i], row_vec(conv_dw_b[i]),
                   row_vec(conv_ln_w[i]), row_vec(conv_ln_b[i]), w_conv_out[i].astype(BF16),
                   w_out[i].astype(BF16), shift2, scale2, gate2, row_vec(norm_ffn_w[i]),
                   row_vec(final_norm_w), w_ffn_in[i].astype(BF16), w_ffn_down[i].astype(BF16))
```

```python
import functools
import math

import jax
import jax.numpy as jnp
from jax import lax
from jax.experimental import pallas as pl
from jax.experimental.pallas import tpu as pltpu

F32 = jnp.float32
BF16 = jnp.bfloat16

N_GROUPS = 4
D_STATE = 128
HEAD_DIM = 64
NORM_EPS = 1e-6
LN_EPS = 1e-5

LANES = 128
SUBLANES = 8
MXU_COLS = 256
SSD_CHUNK = LANES
HEADS_PER_BLOCK = LANES // HEAD_DIM
MASK_BIAS = -1e30
LOG2E = math.log2(math.e)
VMEM_LIMIT = 56 * 1024 * 1024

INPROJ_ROWS = 512
SSD_ROWS = 512
MIXFFN_ROWS = 256
COL_TILE = 512
CONV_ROW_TILE = 64


def _sigmoid(v):
    return jax.nn.sigmoid(v)


def _silu(v):
    return v * jax.nn.sigmoid(v)


def _softplus(v):
    return jnp.maximum(v, 0.0) + jnp.log1p(jnp.exp(-jnp.abs(v)))


def _dot(a, b):
    return jnp.dot(a, b, preferred_element_type=F32)


def _resident(shape):
    nd = len(shape)
    return pl.BlockSpec(shape, lambda *_: (0,) * nd, pipeline_mode=pl.Buffered(1))


def _params(n_axes, flags=None):
    return pltpu.CompilerParams(
        dimension_semantics=("arbitrary",) * n_axes, vmem_limit_bytes=VMEM_LIMIT, flags=flags)


def _round_up(n, m):
    return -(-n // m) * m


def _token(*vals):
    tok = jnp.zeros((SUBLANES, LANES), F32)
    for val in vals:
        bits = lax.bitcast_convert_type(val[-SUBLANES:, -LANES:], jnp.int32)
        bits = lax.shift_right_logical(lax.shift_right_logical(bits, 16), 16)
        tok = tok + lax.bitcast_convert_type(bits, F32)
    return tok


def _wait_on(ref, cols, tok):
    packing = 4 // jnp.dtype(ref.dtype).itemsize
    region = ref[:, cols]
    reps = (region.shape[0] // (SUBLANES * packing), region.shape[1] // LANES)
    t = jnp.tile(jnp.concatenate([tok] * packing, axis=0).astype(ref.dtype), reps)
    ref[:, cols] = region + t


def _ada_kernel(c_ref, w_ref, b_ref, o_ref):
    c = c_ref[...]
    o_ref[...] = jnp.dot(_silu(c), w_ref[...], preferred_element_type=F32,
                         precision=lax.Precision.HIGHEST) + b_ref[...]


def _ada(c, w, b):
    bsz, d = c.shape
    n = w.shape[1]
    tn = d
    return pl.pallas_call(
        _ada_kernel,
        grid=(n // tn,),
        in_specs=[pl.BlockSpec((bsz, d), lambda j: (0, 0)),
                  pl.BlockSpec((d, tn), lambda j: (0, j)),
                  pl.BlockSpec((1, tn), lambda j: (0, j))],
        out_specs=pl.BlockSpec((bsz, tn), lambda j: (0, j)),
        out_shape=jax.ShapeDtypeStruct((bsz, n), F32),
        compiler_params=_params(1),
        name="ada",
    )(c, w, b.reshape(1, n))


def _modulated_rms(x, nw, scale, shift):
    ms = jnp.mean(x * x, axis=-1, keepdims=True)
    h = x * lax.rsqrt(ms + NORM_EPS) * nw
    return h * (1.0 + scale) + shift


def _inproj_kernel(x_ref, shift_ref, scale_ref, nw_ref, w_ref, wdt_ref, dtb_ref,
                   scw_ref, scb_ref,
                   u_ref, z_ref, xbc_ref, dt_ref, gc_ref, gs_ref,
                   h_ref, work_ref, halo_ref, *, d_conv, d_inner, d_xbc):
    tm = x_ref.shape[0]
    taps = scw_ref.shape[0]
    halo = halo_ref.shape[0]

    @pl.when(pl.program_id(1) == 0)
    def _():
        halo_ref[...] = jnp.zeros(halo_ref.shape, F32)

    h = _modulated_rms(x_ref[...], nw_ref[...], scale_ref[...], shift_ref[...])
    h_ref[...] = h.astype(BF16)

    def proj(c0):
        return _dot(h_ref[...], w_ref[:, c0:c0 + COL_TILE])

    d_model = gc_ref.shape[-1]
    z_off = 2 * d_conv
    xbc_off = z_off + d_inner
    gc_off = xbc_off + d_xbc
    gs_off = gc_off + d_model
    first = halo - (taps - 1)

    def glu_task(c0):
        a = proj(c0)
        b = proj(d_conv + c0)
        u_ref[:, c0:c0 + COL_TILE] = (a * _sigmoid(b)).astype(BF16)

    def act_task(out_ref, act, off, c0):
        out_ref[:, c0:c0 + COL_TILE] = act(proj(off + c0)).astype(BF16)

    def xbc_task(j):
        c0 = j * COL_TILE
        cs = slice(c0, c0 + COL_TILE)
        work = work_ref.at[j % 2]
        work[0:halo, :] = halo_ref[:, cs]
        work[halo:halo + tm, :] = proj(xbc_off + c0)
        acc = jnp.zeros((tm, COL_TILE), F32) + scb_ref[:, cs]
        for k in range(taps):
            acc = acc + scw_ref[k:k + 1, cs] * work[first + k:first + k + tm, :]
        xbc_ref[:, cs] = _silu(acc).astype(BF16)
        halo_ref[:, cs] = work[tm:tm + halo, :]

    light = [functools.partial(glu_task, j * COL_TILE) for j in range(d_conv // COL_TILE)]
    light += [functools.partial(act_task, z_ref, _silu, z_off, j * COL_TILE)
              for j in range(d_inner // COL_TILE)]
    light += [functools.partial(act_task, gc_ref, _sigmoid, gc_off, j * COL_TILE)
              for j in range(d_model // COL_TILE)]
    light += [functools.partial(act_task, gs_ref, _sigmoid, gs_off, j * COL_TILE)
              for j in range(d_model // COL_TILE)]
    heavy = [functools.partial(xbc_task, j) for j in range(d_xbc // COL_TILE)]
    per_heavy = -(-len(light) // len(heavy))
    for j, task in enumerate(heavy):
        task()
        for t in light[j * per_heavy:(j + 1) * per_heavy]:
            t()
    dt_raw = _dot(h_ref[...], wdt_ref[...]) + dtb_ref[...]
    dt_ref[...] = _softplus(dt_raw)


def _inproj(x, shift, scale, nw, w_main, w_dt, dt_bias, scw, scb, *, d_conv, d_inner, d_xbc):
    bsz, seq, d = x.shape
    tm = INPROJ_ROWS
    halo = _round_up(scw.shape[0] - 1, SUBLANES)
    kern = functools.partial(_inproj_kernel, d_conv=d_conv, d_inner=d_inner, d_xbc=d_xbc)
    row = lambda w: pl.BlockSpec((None, tm, w), lambda b, i: (b, i, 0))
    vec = pl.BlockSpec((None, 1, d), lambda b, i: (b, 0, 0))
    outs = [(d_conv, BF16), (d_inner, BF16), (d_xbc, BF16), (LANES, F32), (d, BF16), (d, BF16)]
    return pl.pallas_call(
        kern,
        grid=(bsz, seq // tm),
        in_specs=[row(d), vec, vec, _resident((1, d)), _resident(w_main.shape),
                  _resident(w_dt.shape), _resident((1, LANES)),
                  _resident(scw.shape), _resident((1, d_xbc))],
        out_specs=[row(w) for w, _ in outs],
        out_shape=[jax.ShapeDtypeStruct((bsz, seq, w), t) for w, t in outs],
        scratch_shapes=[pltpu.VMEM((tm, d), BF16),
                        pltpu.VMEM((2, tm + halo, COL_TILE), F32),
                        pltpu.VMEM((halo, d_xbc), F32)],
        compiler_params=_params(2),
        name="inproj",
    )(x, shift, scale, nw, w_main, w_dt, dt_bias, scw, scb)


def _split3(v):
    hi = v.astype(BF16)
    r1 = v - hi.astype(F32)
    mid = r1.astype(BF16)
    lo = (r1 - mid.astype(F32)).astype(BF16)
    return hi, mid, lo


def _ssd_kernel(xbc_ref, z_ref, dt_ref, gs_ref, alog_ref, dskip_ref, snw_ref, wso_ref,
                o_ref, st_ref, y_ref, dec_ref):
    tl = xbc_ref.shape[0]
    d_inner = z_ref.shape[1]
    n_blocks = d_inner // LANES
    blocks_per_group = n_blocks // N_GROUPS
    b_off = d_inner
    c_off = d_inner + N_GROUPS * D_STATE
    ct = SSD_CHUNK

    @pl.when(pl.program_id(1) == 0)
    def _():
        st_ref[...] = jnp.zeros(st_ref.shape, F32)

    li = lax.broadcasted_iota(jnp.int32, (ct, ct), 0)
    si = lax.broadcasted_iota(jnp.int32, (ct, ct), 1)
    causal = li >= si
    tri = jnp.where(causal, 1.0, 0.0).astype(BF16)
    mask_bias = jnp.where(causal, 0.0, MASK_BIAS)
    low_half = si < HEAD_DIM

    for c in range(tl // ct):
        dtc = dt_ref[c * ct:(c + 1) * ct, :]
        da = dtc * -jnp.exp(alog_ref[...])
        hi, mid, lo = _split3(da)
        acs = _dot(tri, hi) + _dot(tri, mid) + _dot(tri, lo)
        acs2 = acs * LOG2E
        acs2_t = acs2.T
        dt_t = dtc.T
        dec_ref[c, 0] = acs2
        dec_ref[c, 1] = acs2_t - jnp.log2(dt_t)
        dec_ref[c, 2] = dt_t * jnp.exp2(acs2_t[:, ct - 1:ct] - acs2_t)

    def chunk(c, carry):
        r0 = pl.multiple_of(c * ct, ct)
        rows = pl.ds(r0, ct)
        acs2 = dec_ref[c, 0]
        row2_t = dec_ref[c, 1]
        w_t = dec_ref[c, 2]
        e_last = jnp.exp2(acs2[ct - 1:ct, :])
        for g in range(N_GROUPS):
            cg_b = xbc_ref[rows, c_off + g * D_STATE:c_off + (g + 1) * D_STATE]
            bg_b = xbc_ref[rows, b_off + g * D_STATE:b_off + (g + 1) * D_STATE]
            cb_mat = lax.dot_general(cg_b, bg_b, (((1,), (1,)), ((), ())),
                                     preferred_element_type=F32)
            bg_t = bg_b.astype(F32).T
            st_g = st_ref[g]
            y_off = _dot(cg_b, st_g.astype(BF16))
            new_cols = []
            for q in range(blocks_per_group):
                blk = g * blocks_per_group + q
                h0 = blk * HEADS_PER_BLOCK
                bs = slice(blk * LANES, (blk + 1) * LANES)
                xblk = xbc_ref[rows, bs]
                m_parts, b_parts, e_cols = [], [], []
                for hh in (h0, h0 + 1):
                    colb = jnp.broadcast_to(acs2[:, hh:hh + 1], (ct, ct))
                    m_parts.append(jnp.exp2(colb - row2_t[hh:hh + 1, :] + mask_bias) * cb_mat)
                    b_parts.append(bg_t * w_t[hh:hh + 1, :])
                    e_cols.append(jnp.exp2(colb))
                x_lo = jnp.where(low_half, xblk, jnp.zeros_like(xblk))
                x_hi = jnp.where(low_half, jnp.zeros_like(xblk), xblk)
                x_bd = jnp.concatenate([x_lo, x_hi], axis=0)
                m_cat = jnp.concatenate(m_parts, axis=1).astype(BF16)
                y_diag = _dot(m_cat, x_bd)
                e_blk = jnp.where(low_half, e_cols[0], e_cols[1])
                cs = slice(q * LANES, (q + 1) * LANES)
                y_ref[rows, bs] = (y_diag + e_blk * y_off[:, cs]
                                   + dskip_ref[:, bs] * xblk.astype(F32))
                b_cat = jnp.concatenate(b_parts, axis=1).astype(BF16)
                dec = jnp.where(low_half[0:1, :], e_last[:, h0:h0 + 1], e_last[:, h0 + 1:h0 + 2])
                new_cols.append(st_g[:, cs] * dec + _dot(b_cat, x_bd))
            st_ref[g] = jnp.concatenate(new_cols, axis=1)
        return carry

    lax.fori_loop(0, tl // ct, chunk, 0, unroll=2)

    gw = d_inner // N_GROUPS
    parts = []
    for g in range(N_GROUPS):
        gsl = slice(g * gw, (g + 1) * gw)
        yg = y_ref[:, gsl] * z_ref[:, gsl].astype(F32)
        ms = jnp.mean(yg * yg, axis=-1, keepdims=True)
        parts.append((yg * lax.rsqrt(ms + NORM_EPS) * snw_ref[:, gsl]).astype(BF16))
    yn = jnp.concatenate(parts, axis=1)
    o_ref[...] = (gs_ref[...].astype(F32) * _dot(yn, wso_ref[...])).astype(BF16)


def _ssd_branch(xbc, z, dt, gs, alog_pad, dskip, snw, wso):
    bsz, seq, d = gs.shape
    d_xbc = xbc.shape[-1]
    d_inner = z.shape[-1]
    tl = SSD_ROWS
    row = lambda w: pl.BlockSpec((None, tl, w), lambda b, i: (b, i, 0))
    return pl.pallas_call(
        _ssd_kernel,
        grid=(bsz, seq // tl),
        in_specs=[row(d_xbc), row(d_inner), row(LANES), row(d),
                  _resident((1, LANES)), _resident((1, d_inner)), _resident((1, d_inner)),
                  _resident(wso.shape)],
        out_specs=row(d),
        out_shape=jax.ShapeDtypeStruct((bsz, seq, d), BF16),
        scratch_shapes=[pltpu.VMEM((N_GROUPS, D_STATE, d_inner // N_GROUPS), F32),
                        pltpu.VMEM((tl, d_inner), F32),
                        pltpu.VMEM((tl // SSD_CHUNK, 3, SSD_CHUNK, LANES), F32)],
        compiler_params=_params(2),
        name="ssd",
    )(xbc, z, dt, gs, alog_pad, dskip, snw, wso)


def _mixffn_kernel(u_ref, gc_ref, ysg_ref, x_ref, gate1_ref,
                   cw_ref, cb_ref, lnw_ref, lnb_ref, wco_ref, wout_ref,
                   shift_ref, scale_ref, gate2_ref, nw_ref, fnw_ref, win_ref, wdn_ref,
                   o_ref,
                   ext_ref, ph_ref, x1_ref, h_ref, act_ref, *, halo, slabs):
    tl, d = u_ref.shape
    taps = cw_ref.shape[0]
    ph_rows = ph_ref.shape[2]
    step = pl.program_id(1)

    @pl.when(step == 0)
    def _():
        ext_ref[0:halo, :] = jnp.zeros((halo, d), F32)

    @pl.when((step == 0) & (pl.program_id(0) == 0))
    def _():
        x1_ref[...] = jnp.zeros(x1_ref.shape, F32)
        h_ref[...] = jnp.zeros(h_ref.shape, BF16)


    first = halo - (taps - 1)
    d_ff = wdn_ref.shape[0]

    ext_ref[halo:halo + tl, :] = u_ref[...].astype(F32)

    def conv_block(cb):
        cs = slice(cb * LANES, (cb + 1) * LANES)
        for p in range(1, SUBLANES):
            ph_ref[cb, p - 1] = ext_ref[p:p + ph_rows, cs]
        out = []
        for rb in range(tl // CONV_ROW_TILE):
            r0 = rb * CONV_ROW_TILE
            acc = jnp.zeros((CONV_ROW_TILE, LANES), F32) + cb_ref[:, cs]
            for k in range(taps):
                a8, p = divmod(first + k, SUBLANES)
                lo = r0 + a8 * SUBLANES
                if p == 0:
                    src = ext_ref[lo:lo + CONV_ROW_TILE, cs]
                else:
                    src = ph_ref[cb, p - 1, lo:lo + CONV_ROW_TILE, :]
                acc = acc + cw_ref[k:k + 1, cs] * src
            out.append(acc)
        return jnp.concatenate(out, axis=0)

    conv_blocks = list(range(d // LANES))
    n_stages = 3 * len(slabs)
    bounds = [-(-len(conv_blocks) * k // n_stages) for k in range(n_stages + 1)]
    shares = [conv_blocks[bounds[k]:bounds[k + 1]] for k in range(n_stages)]
    v_blocks = []
    tok = None
    ffn_acc = None

    def conv_stage(k, tok):
        outs = []
        for cb in shares[k]:
            if tok is not None:
                _wait_on(ext_ref, slice(cb * LANES, (cb + 1) * LANES), tok)
            outs.append(conv_block(cb))
        v_blocks.extend(outs)
        return outs[-1:]

    for si, (s0, s1) in enumerate(slabs):
        w = s1 - s0
        if tok is not None:
            _wait_on(h_ref, slice(None), tok)
        g = _dot(h_ref[...], win_ref[:, s0:s1])
        tok = _token(g, *conv_stage(3 * si, tok))

        _wait_on(h_ref, slice(None), tok)
        up = _dot(h_ref[...], win_ref[:, d_ff + s0:d_ff + s1])
        act_ref[:, 0:w] = (_silu(g) * up).astype(BF16)
        tok = _token(up, *conv_stage(3 * si + 1, tok))

        part = _dot(act_ref[:, 0:w], wdn_ref[s0:s1, :])
        ffn_acc = part if ffn_acc is None else ffn_acc + part
        tok = _token(part, *conv_stage(3 * si + 2, tok))

    x2 = x1_ref[...] + gate2_ref[...] * ffn_acc
    ms = jnp.mean(x2 * x2, axis=-1, keepdims=True)
    out = x2 * lax.rsqrt(ms + NORM_EPS) * fnw_ref[...]

    v = jnp.concatenate(v_blocks, axis=1)
    new_halo = ext_ref[tl:tl + halo, :]
    mu = jnp.mean(v, axis=-1, keepdims=True)
    cen = v - mu
    var = jnp.mean(cen * cen, axis=-1, keepdims=True)
    y = cen * lax.rsqrt(var + LN_EPS) * lnw_ref[...] + lnb_ref[...]
    y_conv = _dot(_silu(y).astype(BF16), wco_ref[...])
    merged = gc_ref[...].astype(F32) * y_conv + ysg_ref[...].astype(F32)
    x1 = x_ref[...] + gate1_ref[...] * _dot(merged.astype(BF16), wout_ref[...])
    h = _modulated_rms(x1, nw_ref[...], scale_ref[...], shift_ref[...])

    o_ref[...] = out
    ext_ref[0:halo, :] = new_halo
    x1_ref[...] = x1
    h_ref[...] = h.astype(BF16)


def _mixffn(u, gc, ysg, x, gate1, cw, cb, lnw, lnb, wco, wout,
            shift, scale, gate2, nw, fnw, win, wdn):
    bsz, seq, d = x.shape
    tl = MIXFFN_ROWS
    taps = cw.shape[0]
    halo = _round_up(taps - 1, SUBLANES)
    first = halo - (taps - 1)
    ph_rows = tl + max((first + k) // SUBLANES * SUBLANES
                       for k in range(taps) if (first + k) % SUBLANES)
    d_ff = wdn.shape[0]
    half = (d_ff // MXU_COLS + 1) // 2 * MXU_COLS
    slabs = ((0, half), (half, d_ff))
    slab_w = max(s1 - s0 for s0, s1 in slabs)
    kern = functools.partial(_mixffn_kernel, halo=halo, slabs=slabs)
    n_tiles = seq // tl
    row = pl.BlockSpec((None, tl, d), lambda b, i: (b, jnp.minimum(i, n_tiles - 1), 0))
    out_row = pl.BlockSpec((None, tl, d), lambda b, i: (b, jnp.maximum(i - 1, 0), 0))
    vec = pl.BlockSpec((None, 1, d), lambda b, i: (b, 0, 0))
    r1 = _resident((1, d))
    return pl.pallas_call(
        kern,
        grid=(bsz, n_tiles + 1),
        in_specs=[row, row, row, row, vec,
                  _resident(cw.shape), r1, r1, r1, _resident(wco.shape), _resident(wout.shape),
                  vec, vec, vec, r1, r1, _resident(win.shape), _resident(wdn.shape)],
        out_specs=out_row,
        out_shape=jax.ShapeDtypeStruct((bsz, seq, d), F32),
        scratch_shapes=[pltpu.VMEM((tl + halo, d), F32),
                        pltpu.VMEM((d // LANES, SUBLANES - 1, ph_rows, LANES), F32),
                        pltpu.VMEM((tl, d), F32),
                        pltpu.VMEM((tl, d), BF16),
                        pltpu.VMEM((tl, slab_w), BF16)],
        compiler_params=_params(2),
        name="mixffn",
    )(u, gc, ysg, x, gate1, cw, cb, lnw, lnb, wco, wout, shift, scale, gate2, nw, fnw, win, wdn)


def _mods(c, w, b):
    bsz, d = c.shape
    mod = _ada(c, w, b)
    return [mod[:, i * d:(i + 1) * d].reshape(bsz, 1, d) for i in range(3)]


def _pad_lanes(v, width=LANES):
    return jnp.pad(v, [(0, 0)] * (v.ndim - 1) + [(0, width - v.shape[-1])])


def kernel(x, c, w_ada_mix, b_ada_mix, norm_mix_w, w_in, conv_dw_w, conv_dw_b, conv_ln_w, conv_ln_b, w_conv_out, ssm_conv_w, ssm_conv_b, dt_bias, a_log, d_skip, ssm_norm_w, w_ssm_out, w_out, w_ada_ffn, b_ada_ffn, norm_ffn_w, w_ffn_in, w_ffn_down, final_norm_w):
    bsz, seq, d = x.shape
    depth = w_in.shape[0]
    assert depth == 1, "the last kernel applies the final RMSNorm, so one layer only"
    d_conv = conv_dw_w.shape[-1]
    d_inner = ssm_norm_w.shape[-1]
    d_xbc = ssm_conv_w.shape[-1]
    n_heads = dt_bias.shape[-1]
    assert d_inner == n_heads * HEAD_DIM and d_xbc == d_inner + 2 * N_GROUPS * D_STATE
    assert n_heads <= LANES and seq % max(INPROJ_ROWS, SSD_ROWS, MIXFFN_ROWS) == 0
    row_vec = lambda v: v.reshape(1, -1).astype(F32)
    i = 0

    shift, scale, gate = _mods(c, w_ada_mix[i], b_ada_mix[i])
    w = w_in[i]
    s_dt = 2 * d_conv + d_inner + d_xbc
    w_main = jnp.concatenate([w[:, :s_dt], w[:, s_dt + n_heads:]], axis=1).astype(BF16)
    w_dt = _pad_lanes(w[:, s_dt:s_dt + n_heads]).astype(BF16)
    u, z, xbc, dt, gc, gs = _inproj(
        x, shift, scale, row_vec(norm_mix_w[i]), w_main, w_dt, _pad_lanes(row_vec(dt_bias[i])),
        ssm_conv_w[i], row_vec(ssm_conv_b[i]), d_conv=d_conv, d_inner=d_inner, d_xbc=d_xbc)
    dskip = row_vec(jnp.repeat(d_skip[i].astype(F32), HEAD_DIM))
    ysg = _ssd_branch(xbc, z, dt, gs, _pad_lanes(row_vec(a_log[i])), dskip,
                      row_vec(ssm_norm_w[i]), w_ssm_out[i].astype(BF16))
    shift2, scale2, gate2 = _mods(c, w_ada_ffn[i], b_ada_ffn[i])
    return _mixffn(u, gc, ysg, x, gate, conv_dw_w[i], row_vec(conv_dw_b[i]),
                   row_vec(conv_ln_w[i]), row_vec(conv_ln_b[i]), w_conv_out[i].astype(BF16),
                   w_out[i].astype(BF16), shift2, scale2, gate2, row_vec(norm_ffn_w[i]),
                   row_vec(final_norm_w), w_ffn_in[i].astype(BF16), w_ffn_down[i].astype(BF16))
```

```python
import functools
import math

import jax
import jax.numpy as jnp
from jax import lax
from jax.experimental import pallas as pl
from jax.experimental.pallas import tpu as pltpu

F32 = jnp.float32
BF16 = jnp.bfloat16

N_GROUPS = 4
D_STATE = 128
HEAD_DIM = 64
NORM_EPS = 1e-6
LN_EPS = 1e-5

LANES = 128
SUBLANES = 8
MXU_COLS = 256
SSD_CHUNK = LANES
HEADS_PER_BLOCK = LANES // HEAD_DIM
MASK_BIAS = -1e30
LOG2E = math.log2(math.e)
VMEM_LIMIT = 56 * 1024 * 1024

INPROJ_ROWS = 512
SSD_ROWS = 512
MIXFFN_ROWS = 256
COL_TILE = 512
CONV_ROW_TILE = 64
FFN_SLABS = 2


def _sigmoid(v):
    return jax.nn.sigmoid(v)


def _silu(v):
    return v * jax.nn.sigmoid(v)


def _softplus(v):
    return jnp.maximum(v, 0.0) + jnp.log1p(jnp.exp(-jnp.abs(v)))


def _dot(a, b):
    return jnp.dot(a, b, preferred_element_type=F32)


def _resident(shape):
    nd = len(shape)
    return pl.BlockSpec(shape, lambda *_: (0,) * nd, pipeline_mode=pl.Buffered(1))


def _params(n_axes, flags=None):
    return pltpu.CompilerParams(
        dimension_semantics=("arbitrary",) * n_axes, vmem_limit_bytes=VMEM_LIMIT, flags=flags)


def _round_up(n, m):
    return -(-n // m) * m


def _token(*vals):
    tok = jnp.zeros((SUBLANES, LANES), F32)
    for val in vals:
        bits = lax.bitcast_convert_type(val[-SUBLANES:, -LANES:], jnp.int32)
        bits = lax.shift_right_logical(lax.shift_right_logical(bits, 16), 16)
        tok = tok + lax.bitcast_convert_type(bits, F32)
    return tok


def _wait_on(ref, cols, tok):
    packing = 4 // jnp.dtype(ref.dtype).itemsize
    region = ref[:, cols]
    reps = (region.shape[0] // (SUBLANES * packing), region.shape[1] // LANES)
    t = jnp.tile(jnp.concatenate([tok] * packing, axis=0).astype(ref.dtype), reps)
    ref[:, cols] = region + t


def _ada_kernel(c_ref, w_ref, b_ref, o_ref):
    c = c_ref[...]
    o_ref[...] = jnp.dot(_silu(c), w_ref[...], preferred_element_type=F32,
                         precision=lax.Precision.HIGHEST) + b_ref[...]


def _ada(c, w, b):
    bsz, d = c.shape
    n = w.shape[1]
    tn = d
    return pl.pallas_call(
        _ada_kernel,
        grid=(n // tn,),
        in_specs=[pl.BlockSpec((bsz, d), lambda j: (0, 0)),
                  pl.BlockSpec((d, tn), lambda j: (0, j)),
                  pl.BlockSpec((1, tn), lambda j: (0, j))],
        out_specs=pl.BlockSpec((bsz, tn), lambda j: (0, j)),
        out_shape=jax.ShapeDtypeStruct((bsz, n), F32),
        compiler_params=_params(1),
        name="ada",
    )(c, w, b.reshape(1, n))


def _modulated_rms(x, nw, scale, shift):
    ms = jnp.mean(x * x, axis=-1, keepdims=True)
    h = x * lax.rsqrt(ms + NORM_EPS) * nw
    return h * (1.0 + scale) + shift


def _inproj_kernel(x_ref, shift_ref, scale_ref, nw_ref, wlo_ref, whi_ref, wdt_ref, dtb_ref,
                   scw_ref, scb_ref,
                   u_ref, z_ref, xbc_ref, dt_ref, gc_ref, gs_ref,
                   h_ref, work_ref, halo_ref, *, d_conv, d_inner, d_xbc):
    tm = x_ref.shape[0]
    taps = scw_ref.shape[0]
    halo = halo_ref.shape[0]

    @pl.when(pl.program_id(1) == 0)
    def _():
        halo_ref[...] = jnp.zeros(halo_ref.shape, F32)

    h = _modulated_rms(x_ref[...], nw_ref[...], scale_ref[...], shift_ref[...])
    h_ref[...] = h.astype(BF16)

    def proj(c0):
        split = wlo_ref.shape[1]
        w = wlo_ref[:, c0:c0 + COL_TILE] if c0 < split else whi_ref[:, c0 - split:c0 - split + COL_TILE]
        return _dot(h_ref[...], w)

    d_model = gc_ref.shape[-1]
    z_off = 2 * d_conv
    xbc_off = z_off + d_inner
    gc_off = xbc_off + d_xbc
    gs_off = gc_off + d_model
    first = halo - (taps - 1)

    def glu_task(c0):
        a = proj(c0)
        b = proj(d_conv + c0)
        u_ref[:, c0:c0 + COL_TILE] = (a * _sigmoid(b)).astype(BF16)

    def act_task(out_ref, act, off, c0):
        out_ref[:, c0:c0 + COL_TILE] = act(proj(off + c0)).astype(BF16)

    def xbc_task(j):
        c0 = j * COL_TILE
        cs = slice(c0, c0 + COL_TILE)
        work = work_ref.at[j % 2]
        work[0:halo, :] = halo_ref[:, cs]
        work[halo:halo + tm, :] = proj(xbc_off + c0)
        acc = jnp.zeros((tm, COL_TILE), F32) + scb_ref[:, cs]
        for k in range(taps):
            acc = acc + scw_ref[k:k + 1, cs] * work[first + k:first + k + tm, :]
        xbc_ref[:, cs] = _silu(acc).astype(BF16)
        halo_ref[:, cs] = work[tm:tm + halo, :]

    light = [functools.partial(glu_task, j * COL_TILE) for j in range(d_conv // COL_TILE)]
    light += [functools.partial(act_task, z_ref, _silu, z_off, j * COL_TILE)
              for j in range(d_inner // COL_TILE)]
    light += [functools.partial(act_task, gc_ref, _sigmoid, gc_off, j * COL_TILE)
              for j in range(d_model // COL_TILE)]
    light += [functools.partial(act_task, gs_ref, _sigmoid, gs_off, j * COL_TILE)
              for j in range(d_model // COL_TILE)]
    heavy = [functools.partial(xbc_task, j) for j in range(d_xbc // COL_TILE)]
    per_heavy = -(-len(light) // len(heavy))
    for j, task in enumerate(heavy):
        task()
        for t in light[j * per_heavy:(j + 1) * per_heavy]:
            t()
    dt_raw = _dot(h_ref[...], wdt_ref[...]) + dtb_ref[...]
    dt_ref[...] = _softplus(dt_raw)


def _inproj(x, shift, scale, nw, w_lo, w_hi, w_dt, dt_bias, scw, scb, *, d_conv, d_inner, d_xbc):
    bsz, seq, d = x.shape
    tm = INPROJ_ROWS
    halo = _round_up(scw.shape[0] - 1, SUBLANES)
    kern = functools.partial(_inproj_kernel, d_conv=d_conv, d_inner=d_inner, d_xbc=d_xbc)
    row = lambda w: pl.BlockSpec((None, tm, w), lambda b, i: (b, i, 0))
    vec = pl.BlockSpec((None, 1, d), lambda b, i: (b, 0, 0))
    outs = [(d_conv, BF16), (d_inner, BF16), (d_xbc, BF16), (LANES, F32), (d, BF16), (d, BF16)]
    return pl.pallas_call(
        kern,
        grid=(bsz, seq // tm),
        in_specs=[row(d), vec, vec, _resident((1, d)), _resident(w_lo.shape), _resident(w_hi.shape),
                  _resident(w_dt.shape), _resident((1, LANES)),
                  _resident(scw.shape), _resident((1, d_xbc))],
        out_specs=[row(w) for w, _ in outs],
        out_shape=[jax.ShapeDtypeStruct((bsz, seq, w), t) for w, t in outs],
        scratch_shapes=[pltpu.VMEM((tm, d), BF16),
                        pltpu.VMEM((2, tm + halo, COL_TILE), F32),
                        pltpu.VMEM((halo, d_xbc), F32)],
        compiler_params=_params(2),
        name="inproj",
    )(x, shift, scale, nw, w_lo, w_hi, w_dt, dt_bias, scw, scb)


def _split3(v):
    hi = v.astype(BF16)
    r1 = v - hi.astype(F32)
    mid = r1.astype(BF16)
    lo = (r1 - mid.astype(F32)).astype(BF16)
    return hi, mid, lo


def _ssd_kernel(xbc_ref, z_ref, dt_ref, gs_ref, alog_ref, dskip_ref, snw_ref, wso_ref,
                o_ref, st_ref, y_ref, dec_ref):
    tl = xbc_ref.shape[0]
    d_inner = z_ref.shape[1]
    n_blocks = d_inner // LANES
    blocks_per_group = n_blocks // N_GROUPS
    b_off = d_inner
    c_off = d_inner + N_GROUPS * D_STATE
    ct = SSD_CHUNK

    @pl.when(pl.program_id(1) == 0)
    def _():
        st_ref[...] = jnp.zeros(st_ref.shape, F32)

    li = lax.broadcasted_iota(jnp.int32, (ct, ct), 0)
    si = lax.broadcasted_iota(jnp.int32, (ct, ct), 1)
    causal = li >= si
    tri = jnp.where(causal, 1.0, 0.0).astype(BF16)
    mask_bias = jnp.where(causal, 0.0, MASK_BIAS)
    low_half = si < HEAD_DIM

    for c in range(tl // ct):
        dtc = dt_ref[c * ct:(c + 1) * ct, :]
        da = dtc * -jnp.exp(alog_ref[...])
        hi, mid, lo = _split3(da)
        acs = _dot(tri, hi) + _dot(tri, mid) + _dot(tri, lo)
        acs2 = acs * LOG2E
        acs2_t = acs2.T
        dt_t = dtc.T
        dec_ref[c, 0] = acs2
        dec_ref[c, 1] = acs2_t - jnp.log2(dt_t)
        dec_ref[c, 2] = dt_t * jnp.exp2(acs2_t[:, ct - 1:ct] - acs2_t)

    def chunk(c, carry):
        r0 = pl.multiple_of(c * ct, ct)
        rows = pl.ds(r0, ct)
        acs2 = dec_ref[c, 0]
        row2_t = dec_ref[c, 1]
        w_t = dec_ref[c, 2]
        e_last = jnp.exp2(acs2[ct - 1:ct, :])
        for g in range(N_GROUPS):
            cg_b = xbc_ref[rows, c_off + g * D_STATE:c_off + (g + 1) * D_STATE]
            bg_b = xbc_ref[rows, b_off + g * D_STATE:b_off + (g + 1) * D_STATE]
            cb_mat = lax.dot_general(cg_b, bg_b, (((1,), (1,)), ((), ())),
                                     preferred_element_type=F32)
            bg_t = bg_b.astype(F32).T
            st_g = st_ref[g]
            y_off = _dot(cg_b, st_g.astype(BF16))
            new_cols = []
            for q in range(blocks_per_group):
                blk = g * blocks_per_group + q
                h0 = blk * HEADS_PER_BLOCK
                bs = slice(blk * LANES, (blk + 1) * LANES)
                xblk = xbc_ref[rows, bs]
                m_parts, b_parts, e_cols = [], [], []
                for hh in (h0, h0 + 1):
                    colb = jnp.broadcast_to(acs2[:, hh:hh + 1], (ct, ct))
                    m_parts.append(jnp.exp2(colb - row2_t[hh:hh + 1, :] + mask_bias) * cb_mat)
                    b_parts.append(bg_t * w_t[hh:hh + 1, :])
                    e_cols.append(jnp.exp2(colb))
                x_lo = jnp.where(low_half, xblk, jnp.zeros_like(xblk))
                x_hi = jnp.where(low_half, jnp.zeros_like(xblk), xblk)
                x_bd = jnp.concatenate([x_lo, x_hi], axis=0)
                m_cat = jnp.concatenate(m_parts, axis=1).astype(BF16)
                y_diag = _dot(m_cat, x_bd)
                e_blk = jnp.where(low_half, e_cols[0], e_cols[1])
                cs = slice(q * LANES, (q + 1) * LANES)
                y_ref[rows, bs] = (y_diag + e_blk * y_off[:, cs]
                                   + dskip_ref[:, bs] * xblk.astype(F32))
                b_cat = jnp.concatenate(b_parts, axis=1).astype(BF16)
                dec = jnp.where(low_half[0:1, :], e_last[:, h0:h0 + 1], e_last[:, h0 + 1:h0 + 2])
                new_cols.append(st_g[:, cs] * dec + _dot(b_cat, x_bd))
            st_ref[g] = jnp.concatenate(new_cols, axis=1)
        return carry

    lax.fori_loop(0, tl // ct, chunk, 0, unroll=True)

    gw = d_inner // N_GROUPS
    parts = []
    for g in range(N_GROUPS):
        gsl = slice(g * gw, (g + 1) * gw)
        yg = y_ref[:, gsl] * z_ref[:, gsl].astype(F32)
        ms = jnp.mean(yg * yg, axis=-1, keepdims=True)
        parts.append((yg * lax.rsqrt(ms + NORM_EPS) * snw_ref[:, gsl]).astype(BF16))
    yn = jnp.concatenate(parts, axis=1)
    o_ref[...] = (gs_ref[...].astype(F32) * _dot(yn, wso_ref[...])).astype(BF16)


def _ssd_branch(xbc, z, dt, gs, alog_pad, dskip, snw, wso):
    bsz, seq, d = gs.shape
    d_xbc = xbc.shape[-1]
    d_inner = z.shape[-1]
    tl = SSD_ROWS
    row = lambda w: pl.BlockSpec((None, tl, w), lambda b, i: (b, i, 0))
    return pl.pallas_call(
        _ssd_kernel,
        grid=(bsz, seq // tl),
        in_specs=[row(d_xbc), row(d_inner), row(LANES), row(d),
                  _resident((1, LANES)), _resident((1, d_inner)), _resident((1, d_inner)),
                  _resident(wso.shape)],
        out_specs=row(d),
        out_shape=jax.ShapeDtypeStruct((bsz, seq, d), BF16),
        scratch_shapes=[pltpu.VMEM((N_GROUPS, D_STATE, d_inner // N_GROUPS), F32),
                        pltpu.VMEM((tl, d_inner), F32),
                        pltpu.VMEM((tl // SSD_CHUNK, 3, SSD_CHUNK, LANES), F32)],
        compiler_params=_params(2),
        name="ssd",
    )(xbc, z, dt, gs, alog_pad, dskip, snw, wso)


def _mixffn_kernel(u_ref, gc_ref, ysg_ref, x_ref, gate1_ref,
                   cw_ref, cb_ref, lnw_ref, lnb_ref, wco_ref, wout_ref,
                   shift_ref, scale_ref, gate2_ref, nw_ref, fnw_ref, win_ref, wdn_ref,
                   o_ref,
                   ext_ref, ph_ref, x1_ref, h_ref, act_ref, *, halo, slabs):
    tl, d = u_ref.shape
    taps = cw_ref.shape[0]
    ph_rows = ph_ref.shape[2]
    step = pl.program_id(1)

    @pl.when(step == 0)
    def _():
        ext_ref[0:halo, :] = jnp.zeros((halo, d), F32)

    @pl.when((step == 0) & (pl.program_id(0) == 0))
    def _():
        x1_ref[...] = jnp.zeros(x1_ref.shape, F32)
        h_ref[...] = jnp.zeros(h_ref.shape, BF16)


    first = halo - (taps - 1)
    d_ff = wdn_ref.shape[0]

    ext_ref[halo:halo + tl, :] = u_ref[...].astype(F32)

    def conv_block(cb):
        cs = slice(cb * LANES, (cb + 1) * LANES)
        for p in range(1, SUBLANES):
            ph_ref[cb, p - 1] = ext_ref[p:p + ph_rows, cs]
        out = []
        for rb in range(tl // CONV_ROW_TILE):
            r0 = rb * CONV_ROW_TILE
            acc = jnp.zeros((CONV_ROW_TILE, LANES), F32) + cb_ref[:, cs]
            for k in range(taps):
                a8, p = divmod(first + k, SUBLANES)
                lo = r0 + a8 * SUBLANES
                if p == 0:
                    src = ext_ref[lo:lo + CONV_ROW_TILE, cs]
                else:
                    src = ph_ref[cb, p - 1, lo:lo + CONV_ROW_TILE, :]
                acc = acc + cw_ref[k:k + 1, cs] * src
            out.append(acc)
        return jnp.concatenate(out, axis=0)

    conv_blocks = list(range(d // LANES))
    n_stages = 3 * len(slabs)
    bounds = [-(-len(conv_blocks) * k // n_stages) for k in range(n_stages + 1)]
    shares = [conv_blocks[bounds[k]:bounds[k + 1]] for k in range(n_stages)]
    v_blocks = []
    tok = None
    ffn_acc = None

    def conv_stage(k, tok):
        outs = []
        for cb in shares[k]:
            if tok is not None:
                _wait_on(ext_ref, slice(cb * LANES, (cb + 1) * LANES), tok)
            outs.append(conv_block(cb))
        v_blocks.extend(outs)
        return outs[-1:]

    for si, (s0, s1) in enumerate(slabs):
        w = s1 - s0
        if tok is not None:
            _wait_on(h_ref, slice(None), tok)
        g = _dot(h_ref[...], win_ref[:, s0:s1])
        tok = _token(g, *conv_stage(3 * si, tok))

        up = _dot(h_ref[...], win_ref[:, d_ff + s0:d_ff + s1])
        act_ref[:, 0:w] = (_silu(g) * up).astype(BF16)
        tok = _token(up, *conv_stage(3 * si + 1, tok))

        part = _dot(act_ref[:, 0:w], wdn_ref[s0:s1, :])
        ffn_acc = part if ffn_acc is None else ffn_acc + part
        tok = _token(part, *conv_stage(3 * si + 2, tok))

    x2 = x1_ref[...] + gate2_ref[...] * ffn_acc
    ms = jnp.mean(x2 * x2, axis=-1, keepdims=True)
    out = x2 * lax.rsqrt(ms + NORM_EPS) * fnw_ref[...]

    v = jnp.concatenate(v_blocks, axis=1)
    new_halo = ext_ref[tl:tl + halo, :]
    mu = jnp.mean(v, axis=-1, keepdims=True)
    cen = v - mu
    var = jnp.mean(cen * cen, axis=-1, keepdims=True)
    y = cen * lax.rsqrt(var + LN_EPS) * lnw_ref[...] + lnb_ref[...]
    y_conv = _dot(_silu(y).astype(BF16), wco_ref[...])
    merged = gc_ref[...].astype(F32) * y_conv + ysg_ref[...].astype(F32)
    x1 = x_ref[...] + gate1_ref[...] * _dot(merged.astype(BF16), wout_ref[...])
    h = _modulated_rms(x1, nw_ref[...], scale_ref[...], shift_ref[...])

    o_ref[...] = out
    ext_ref[0:halo, :] = new_halo
    x1_ref[...] = x1
    h_ref[...] = h.astype(BF16)


def _mixffn(u, gc, ysg, x, gate1, cw, cb, lnw, lnb, wco, wout,
            shift, scale, gate2, nw, fnw, win, wdn):
    bsz, seq, d = x.shape
    tl = MIXFFN_ROWS
    taps = cw.shape[0]
    halo = _round_up(taps - 1, SUBLANES)
    first = halo - (taps - 1)
    ph_rows = tl + max((first + k) // SUBLANES * SUBLANES
                       for k in range(taps) if (first + k) % SUBLANES)
    d_ff = wdn.shape[0]
    col_tiles = -(-d_ff // MXU_COLS)
    cuts = [min(-(-col_tiles * k // FFN_SLABS) * MXU_COLS, d_ff) for k in range(FFN_SLABS + 1)]
    slabs = tuple(zip(cuts[:-1], cuts[1:]))
    slab_w = max(s1 - s0 for s0, s1 in slabs)
    kern = functools.partial(_mixffn_kernel, halo=halo, slabs=slabs)
    n_tiles = seq // tl
    row = pl.BlockSpec((None, tl, d), lambda b, i: (b, jnp.minimum(i, n_tiles - 1), 0))
    out_row = pl.BlockSpec((None, tl, d), lambda b, i: (b, jnp.maximum(i - 1, 0), 0))
    vec = pl.BlockSpec((None, 1, d), lambda b, i: (b, 0, 0))
    r1 = _resident((1, d))
    return pl.pallas_call(
        kern,
        grid=(bsz, n_tiles + 1),
        in_specs=[row, row, row, row, vec,
                  _resident(cw.shape), r1, r1, r1, _resident(wco.shape), _resident(wout.shape),
                  vec, vec, vec, r1, r1, _resident(win.shape), _resident(wdn.shape)],
        out_specs=out_row,
        out_shape=jax.ShapeDtypeStruct((bsz, seq, d), F32),
        scratch_shapes=[pltpu.VMEM((tl + halo, d), F32),
                        pltpu.VMEM((d // LANES, SUBLANES - 1, ph_rows, LANES), F32),
                        pltpu.VMEM((tl, d), F32),
                        pltpu.VMEM((tl, d), BF16),
                        pltpu.VMEM((tl, slab_w), BF16)],
        compiler_params=_params(2),
        name="mixffn",
    )(u, gc, ysg, x, gate1, cw, cb, lnw, lnb, wco, wout, shift, scale, gate2, nw, fnw, win, wdn)


def _mods(c, w, b):
    bsz, d = c.shape
    mod = _ada(c, w, b)
    return [mod[:, i * d:(i + 1) * d].reshape(bsz, 1, d) for i in range(3)]


def _pad_lanes(v, width=LANES):
    return jnp.pad(v, [(0, 0)] * (v.ndim - 1) + [(0, width - v.shape[-1])])


def kernel(x, c, w_ada_mix, b_ada_mix, norm_mix_w, w_in, conv_dw_w, conv_dw_b, conv_ln_w, conv_ln_b, w_conv_out, ssm_conv_w, ssm_conv_b, dt_bias, a_log, d_skip, ssm_norm_w, w_ssm_out, w_out, w_ada_ffn, b_ada_ffn, norm_ffn_w, w_ffn_in, w_ffn_down, final_norm_w):
    bsz, seq, d = x.shape
    depth = w_in.shape[0]
    assert depth == 1, "the last kernel applies the final RMSNorm, so one layer only"
    d_conv = conv_dw_w.shape[-1]
    d_inner = ssm_norm_w.shape[-1]
    d_xbc = ssm_conv_w.shape[-1]
    n_heads = dt_bias.shape[-1]
    assert d_inner == n_heads * HEAD_DIM and d_xbc == d_inner + 2 * N_GROUPS * D_STATE
    assert n_heads <= LANES and seq % max(INPROJ_ROWS, SSD_ROWS, MIXFFN_ROWS) == 0
    row_vec = lambda v: v.reshape(1, -1).astype(F32)
    i = 0

    shift, scale, gate = _mods(c, w_ada_mix[i], b_ada_mix[i])
    w = w_in[i]
    s_dt = 2 * d_conv + d_inner + d_xbc
    w_lo = w[:, :s_dt].astype(BF16)
    w_hi = w[:, s_dt + n_heads:].astype(BF16)
    w_dt = _pad_lanes(w[:, s_dt:s_dt + n_heads]).astype(BF16)
    u, z, xbc, dt, gc, gs = _inproj(
        x, shift, scale, row_vec(norm_mix_w[i]), w_lo, w_hi, w_dt, _pad_lanes(row_vec(dt_bias[i])),
        ssm_conv_w[i], row_vec(ssm_conv_b[i]), d_conv=d_conv, d_inner=d_inner, d_xbc=d_xbc)
    dskip = row_vec(jnp.repeat(d_skip[i].astype(F32), HEAD_DIM))
    ysg = _ssd_branch(xbc, z, dt, gs, _pad_lanes(row_vec(a_log[i])), dskip,
                      row_vec(ssm_norm_w[i]), w_ssm_out[i].astype(BF16))
    shift2, scale2, gate2 = _mods(c, w_ada_ffn[i], b_ada_ffn[i])
    return _mixffn(u, gc, ysg, x, gate, conv_dw_w[i], row_vec(conv_dw_b[i]),
                   row_vec(conv_ln_w[i]), row_vec(conv_ln_b[i]), w_conv_out[i].astype(BF16),
                   w_out[i].astype(BF16), shift2, scale2, gate2, row_vec(norm_ffn_w[i]),
                   row_vec(final_norm_w), w_ffn_in[i].astype(BF16), w_ffn_down[i].astype(BF16))
```

```python
import functools
import math

import jax
import jax.numpy as jnp
from jax import lax
from jax.experimental import pallas as pl
from jax.experimental.pallas import tpu as pltpu

F32 = jnp.float32
BF16 = jnp.bfloat16

N_GROUPS = 4
D_STATE = 128
HEAD_DIM = 64
NORM_EPS = 1e-6
LN_EPS = 1e-5

LANES = 128
SUBLANES = 8
MXU_COLS = 256
SSD_CHUNK = LANES
HEADS_PER_BLOCK = LANES // HEAD_DIM
MASK_BIAS = -1e30
LOG2E = math.log2(math.e)
VMEM_LIMIT = 56 * 1024 * 1024

INPROJ_ROWS = 512
SSD_ROWS = 1024
MIXFFN_ROWS = 512
COL_TILE = 512
CONV_ROW_TILE = 64
FFN_SLABS = 2


def _sigmoid(v):
    return jax.nn.sigmoid(v)


def _silu(v):
    return v * jax.nn.sigmoid(v)


def _softplus(v):
    return jnp.maximum(v, 0.0) + jnp.log1p(jnp.exp(-jnp.abs(v)))


def _dot(a, b):
    return jnp.dot(a, b, preferred_element_type=F32)


def _resident(shape):
    nd = len(shape)
    return pl.BlockSpec(shape, lambda *_: (0,) * nd, pipeline_mode=pl.Buffered(1))


def _params(n_axes, flags=None):
    return pltpu.CompilerParams(
        dimension_semantics=("arbitrary",) * n_axes, vmem_limit_bytes=VMEM_LIMIT, flags=flags)


def _round_up(n, m):
    return -(-n // m) * m


def _token(*vals):
    tok = jnp.zeros((SUBLANES, LANES), F32)
    for val in vals:
        bits = lax.bitcast_convert_type(val[-SUBLANES:, -LANES:], jnp.int32)
        bits = lax.shift_right_logical(lax.shift_right_logical(bits, 16), 16)
        tok = tok + lax.bitcast_convert_type(bits, F32)
    return tok


def _wait_on(ref, cols, tok):
    packing = 4 // jnp.dtype(ref.dtype).itemsize
    region = ref[:, cols]
    reps = (region.shape[0] // (SUBLANES * packing), region.shape[1] // LANES)
    t = jnp.tile(jnp.concatenate([tok] * packing, axis=0).astype(ref.dtype), reps)
    ref[:, cols] = region + t


def _ada_kernel(c_ref, w_ref, b_ref, o_ref):
    c = c_ref[...]
    o_ref[...] = jnp.dot(_silu(c), w_ref[...], preferred_element_type=F32,
                         precision=lax.Precision.HIGHEST) + b_ref[...]


def _ada(c, w, b):
    bsz, d = c.shape
    n = w.shape[1]
    tn = d
    return pl.pallas_call(
        _ada_kernel,
        grid=(n // tn,),
        in_specs=[pl.BlockSpec((bsz, d), lambda j: (0, 0)),
                  pl.BlockSpec((d, tn), lambda j: (0, j)),
                  pl.BlockSpec((1, tn), lambda j: (0, j))],
        out_specs=pl.BlockSpec((bsz, tn), lambda j: (0, j)),
        out_shape=jax.ShapeDtypeStruct((bsz, n), F32),
        compiler_params=_params(1),
        name="ada",
    )(c, w, b.reshape(1, n))


def _modulated_rms(x, nw, scale, shift):
    ms = jnp.mean(x * x, axis=-1, keepdims=True)
    h = x * lax.rsqrt(ms + NORM_EPS) * nw
    return h * (1.0 + scale) + shift


def _inproj_kernel(x_ref, shift_ref, scale_ref, nw_ref, wlo_ref, whi_ref, wdt_ref, dtb_ref,
                   scw_ref, scb_ref,
                   u_ref, z_ref, xbc_ref, dt_ref, gc_ref, gs_ref,
                   h_ref, work_ref, halo_ref, *, d_conv, d_inner, d_xbc):
    tm = x_ref.shape[0]
    taps = scw_ref.shape[0]
    halo = halo_ref.shape[0]

    @pl.when(pl.program_id(1) == 0)
    def _():
        halo_ref[...] = jnp.zeros(halo_ref.shape, F32)

    h = _modulated_rms(x_ref[...], nw_ref[...], scale_ref[...], shift_ref[...])
    h_ref[...] = h.astype(BF16)

    def proj(c0):
        split = wlo_ref.shape[1]
        w = wlo_ref[:, c0:c0 + COL_TILE] if c0 < split else whi_ref[:, c0 - split:c0 - split + COL_TILE]
        return _dot(h_ref[...], w)

    d_model = gc_ref.shape[-1]
    z_off = 2 * d_conv
    xbc_off = z_off + d_inner
    gc_off = xbc_off + d_xbc
    gs_off = gc_off + d_model
    first = halo - (taps - 1)

    def glu_task(c0):
        a = proj(c0)
        b = proj(d_conv + c0)
        u_ref[:, c0:c0 + COL_TILE] = (a * _sigmoid(b)).astype(BF16)

    def act_task(out_ref, act, off, c0):
        out_ref[:, c0:c0 + COL_TILE] = act(proj(off + c0)).astype(BF16)

    def xbc_task(j):
        c0 = j * COL_TILE
        cs = slice(c0, c0 + COL_TILE)
        work = work_ref.at[j % 2]
        work[0:halo, :] = halo_ref[:, cs]
        work[halo:halo + tm, :] = proj(xbc_off + c0)
        acc = jnp.zeros((tm, COL_TILE), F32) + scb_ref[:, cs]
        for k in range(taps):
            acc = acc + scw_ref[k:k + 1, cs] * work[first + k:first + k + tm, :]
        xbc_ref[:, cs] = _silu(acc).astype(BF16)
        halo_ref[:, cs] = work[tm:tm + halo, :]

    light = [functools.partial(glu_task, j * COL_TILE) for j in range(d_conv // COL_TILE)]
    light += [functools.partial(act_task, z_ref, _silu, z_off, j * COL_TILE)
              for j in range(d_inner // COL_TILE)]
    light += [functools.partial(act_task, gc_ref, _sigmoid, gc_off, j * COL_TILE)
              for j in range(d_model // COL_TILE)]
    light += [functools.partial(act_task, gs_ref, _sigmoid, gs_off, j * COL_TILE)
              for j in range(d_model // COL_TILE)]
    heavy = [functools.partial(xbc_task, j) for j in range(d_xbc // COL_TILE)]
    per_heavy = -(-len(light) // len(heavy))
    for j, task in enumerate(heavy):
        task()
        for t in light[j * per_heavy:(j + 1) * per_heavy]:
            t()
    dt_raw = _dot(h_ref[...], wdt_ref[...]) + dtb_ref[...]
    dt_ref[...] = _softplus(dt_raw)


def _inproj(x, shift, scale, nw, w_lo, w_hi, w_dt, dt_bias, scw, scb, *, d_conv, d_inner, d_xbc):
    bsz, seq, d = x.shape
    tm = INPROJ_ROWS
    halo = _round_up(scw.shape[0] - 1, SUBLANES)
    kern = functools.partial(_inproj_kernel, d_conv=d_conv, d_inner=d_inner, d_xbc=d_xbc)
    row = lambda w: pl.BlockSpec((None, tm, w), lambda b, i: (b, i, 0))
    vec = pl.BlockSpec((None, 1, d), lambda b, i: (b, 0, 0))
    outs = [(d_conv, BF16), (d_inner, BF16), (d_xbc, BF16), (LANES, F32), (d, BF16), (d, BF16)]
    return pl.pallas_call(
        kern,
        grid=(bsz, seq // tm),
        in_specs=[row(d), vec, vec, _resident((1, d)), _resident(w_lo.shape), _resident(w_hi.shape),
                  _resident(w_dt.shape), _resident((1, LANES)),
                  _resident(scw.shape), _resident((1, d_xbc))],
        out_specs=[row(w) for w, _ in outs],
        out_shape=[jax.ShapeDtypeStruct((bsz, seq, w), t) for w, t in outs],
        scratch_shapes=[pltpu.VMEM((tm, d), BF16),
                        pltpu.VMEM((2, tm + halo, COL_TILE), F32),
                        pltpu.VMEM((halo, d_xbc), F32)],
        compiler_params=_params(2),
        name="inproj",
    )(x, shift, scale, nw, w_lo, w_hi, w_dt, dt_bias, scw, scb)


def _split3(v):
    hi = v.astype(BF16)
    r1 = v - hi.astype(F32)
    mid = r1.astype(BF16)
    lo = (r1 - mid.astype(F32)).astype(BF16)
    return hi, mid, lo


def _ssd_kernel(xbc_ref, z_ref, dt_ref, gs_ref, alog_ref, dskip_ref, snw_ref, wso_ref,
                o_ref, st_ref, y_ref, dec_ref):
    tl = xbc_ref.shape[0]
    d_inner = z_ref.shape[1]
    n_blocks = d_inner // LANES
    blocks_per_group = n_blocks // N_GROUPS
    b_off = d_inner
    c_off = d_inner + N_GROUPS * D_STATE
    ct = SSD_CHUNK

    @pl.when(pl.program_id(1) == 0)
    def _():
        st_ref[...] = jnp.zeros(st_ref.shape, F32)

    li = lax.broadcasted_iota(jnp.int32, (ct, ct), 0)
    si = lax.broadcasted_iota(jnp.int32, (ct, ct), 1)
    causal = li >= si
    tri = jnp.where(causal, 1.0, 0.0).astype(BF16)
    mask_bias = jnp.where(causal, 0.0, MASK_BIAS)
    low_half = si < HEAD_DIM

    for c in range(tl // ct):
        dtc = dt_ref[c * ct:(c + 1) * ct, :]
        da = dtc * -jnp.exp(alog_ref[...])
        hi, mid, lo = _split3(da)
        acs = _dot(tri, hi) + _dot(tri, mid) + _dot(tri, lo)
        acs2 = acs * LOG2E
        acs2_t = acs2.T
        dt_t = dtc.T
        dec_ref[c, 0] = acs2
        dec_ref[c, 1] = acs2_t - jnp.log2(dt_t)
        dec_ref[c, 2] = dt_t * jnp.exp2(acs2_t[:, ct - 1:ct] - acs2_t)

    def chunk(c, carry):
        r0 = pl.multiple_of(c * ct, ct)
        rows = pl.ds(r0, ct)
        acs2 = dec_ref[c, 0]
        row2_t = dec_ref[c, 1]
        w_t = dec_ref[c, 2]
        e_last = jnp.exp2(acs2[ct - 1:ct, :])
        for g in range(N_GROUPS):
            cg_b = xbc_ref[rows, c_off + g * D_STATE:c_off + (g + 1) * D_STATE]
            bg_b = xbc_ref[rows, b_off + g * D_STATE:b_off + (g + 1) * D_STATE]
            cb_mat = lax.dot_general(cg_b, bg_b, (((1,), (1,)), ((), ())),
                                     preferred_element_type=F32)
            bg_t = bg_b.astype(F32).T
            st_g = st_ref[g]
            y_off = _dot(cg_b, st_g.astype(BF16))
            new_cols = []
            for q in range(blocks_per_group):
                blk = g * blocks_per_group + q
                h0 = blk * HEADS_PER_BLOCK
                bs = slice(blk * LANES, (blk + 1) * LANES)
                xblk = xbc_ref[rows, bs]
                m_parts, b_parts, e_cols = [], [], []
                for hh in (h0, h0 + 1):
                    colb = jnp.broadcast_to(acs2[:, hh:hh + 1], (ct, ct))
                    m_parts.append(jnp.exp2(colb - row2_t[hh:hh + 1, :] + mask_bias) * cb_mat)
                    b_parts.append(bg_t * w_t[hh:hh + 1, :])
                    e_cols.append(jnp.exp2(colb))
                x_lo = jnp.where(low_half, xblk, jnp.zeros_like(xblk))
                x_hi = jnp.where(low_half, jnp.zeros_like(xblk), xblk)
                x_bd = jnp.concatenate([x_lo, x_hi], axis=0)
                m_cat = jnp.concatenate(m_parts, axis=1).astype(BF16)
                y_diag = _dot(m_cat, x_bd)
                e_blk = jnp.where(low_half, e_cols[0], e_cols[1])
                cs = slice(q * LANES, (q + 1) * LANES)
                y_ref[rows, bs] = (y_diag + e_blk * y_off[:, cs]
                                   + dskip_ref[:, bs] * xblk.astype(F32))
                b_cat = jnp.concatenate(b_parts, axis=1).astype(BF16)
                dec = jnp.where(low_half[0:1, :], e_last[:, h0:h0 + 1], e_last[:, h0 + 1:h0 + 2])
                new_cols.append(st_g[:, cs] * dec + _dot(b_cat, x_bd))
            st_ref[g] = jnp.concatenate(new_cols, axis=1)
        return carry

    lax.fori_loop(0, tl // ct, chunk, 0, unroll=2)

    gw = d_inner // N_GROUPS
    parts = []
    for g in range(N_GROUPS):
        gsl = slice(g * gw, (g + 1) * gw)
        yg = y_ref[:, gsl] * z_ref[:, gsl].astype(F32)
        ms = jnp.mean(yg * yg, axis=-1, keepdims=True)
        parts.append((yg * lax.rsqrt(ms + NORM_EPS) * snw_ref[:, gsl]).astype(BF16))
    yn = jnp.concatenate(parts, axis=1)
    o_ref[...] = (gs_ref[...].astype(F32) * _dot(yn, wso_ref[...])).astype(BF16)


def _ssd_branch(xbc, z, dt, gs, alog_pad, dskip, snw, wso):
    bsz, seq, d = gs.shape
    d_xbc = xbc.shape[-1]
    d_inner = z.shape[-1]
    tl = SSD_ROWS
    row = lambda w: pl.BlockSpec((None, tl, w), lambda b, i: (b, i, 0))
    return pl.pallas_call(
        _ssd_kernel,
        grid=(bsz, seq // tl),
        in_specs=[row(d_xbc), row(d_inner), row(LANES), row(d),
                  _resident((1, LANES)), _resident((1, d_inner)), _resident((1, d_inner)),
                  _resident(wso.shape)],
        out_specs=row(d),
        out_shape=jax.ShapeDtypeStruct((bsz, seq, d), BF16),
        scratch_shapes=[pltpu.VMEM((N_GROUPS, D_STATE, d_inner // N_GROUPS), F32),
                        pltpu.VMEM((tl, d_inner), F32),
                        pltpu.VMEM((tl // SSD_CHUNK, 3, SSD_CHUNK, LANES), F32)],
        compiler_params=_params(2),
        name="ssd",
    )(xbc, z, dt, gs, alog_pad, dskip, snw, wso)


def _mixffn_kernel(u_ref, gc_ref, ysg_ref, x_ref, gate1_ref,
                   cw_ref, cb_ref, lnw_ref, lnb_ref, wco_ref, wout_ref,
                   shift_ref, scale_ref, gate2_ref, nw_ref, fnw_ref, win_ref, wdn_ref,
                   o_ref,
                   ext_ref, ph_ref, x1_ref, h_ref, act_ref, *, halo, slabs):
    tl, d = u_ref.shape
    taps = cw_ref.shape[0]
    ph_rows = ph_ref.shape[2]
    step = pl.program_id(1)

    @pl.when(step == 0)
    def _():
        ext_ref[0:halo, :] = jnp.zeros((halo, d), F32)

    @pl.when((step == 0) & (pl.program_id(0) == 0))
    def _():
        x1_ref[...] = jnp.zeros(x1_ref.shape, F32)
        h_ref[...] = jnp.zeros(h_ref.shape, BF16)


    first = halo - (taps - 1)
    d_ff = wdn_ref.shape[0]

    ext_ref[halo:halo + tl, :] = u_ref[...].astype(F32)

    def conv_block(cb):
        cs = slice(cb * LANES, (cb + 1) * LANES)
        for p in range(1, SUBLANES):
            ph_ref[cb % 2, p - 1] = ext_ref[p:p + ph_rows, cs]
        out = []
        for rb in range(tl // CONV_ROW_TILE):
            r0 = rb * CONV_ROW_TILE
            acc = jnp.zeros((CONV_ROW_TILE, LANES), F32) + cb_ref[:, cs]
            for k in range(taps):
                a8, p = divmod(first + k, SUBLANES)
                lo = r0 + a8 * SUBLANES
                if p == 0:
                    src = ext_ref[lo:lo + CONV_ROW_TILE, cs]
                else:
                    src = ph_ref[cb % 2, p - 1, lo:lo + CONV_ROW_TILE, :]
                acc = acc + cw_ref[k:k + 1, cs] * src
            out.append(acc)
        return jnp.concatenate(out, axis=0)

    conv_blocks = list(range(d // LANES))
    n_stages = 3 * len(slabs)
    bounds = [-(-len(conv_blocks) * k // n_stages) for k in range(n_stages + 1)]
    shares = [conv_blocks[bounds[k]:bounds[k + 1]] for k in range(n_stages)]
    v_blocks = []
    tok = None
    ffn_acc = None

    def conv_stage(k, tok):
        outs = []
        for cb in shares[k]:
            if tok is not None:
                _wait_on(ext_ref, slice(cb * LANES, (cb + 1) * LANES), tok)
            outs.append(conv_block(cb))
        v_blocks.extend(outs)
        return outs[-1:]

    for si, (s0, s1) in enumerate(slabs):
        w = s1 - s0
        if tok is not None:
            _wait_on(h_ref, slice(None), tok)
        g = _dot(h_ref[...], win_ref[:, s0:s1])
        tok = _token(g, *conv_stage(3 * si, tok))

        up = _dot(h_ref[...], win_ref[:, d_ff + s0:d_ff + s1])
        act_ref[:, 0:w] = (_silu(g) * up).astype(BF16)
        tok = _token(up, *conv_stage(3 * si + 1, tok))

        part = _dot(act_ref[:, 0:w], wdn_ref[s0:s1, :])
        ffn_acc = part if ffn_acc is None else ffn_acc + part
        tok = _token(part, *conv_stage(3 * si + 2, tok))

    x2 = x1_ref[...] + gate2_ref[...] * ffn_acc
    ms = jnp.mean(x2 * x2, axis=-1, keepdims=True)
    out = x2 * lax.rsqrt(ms + NORM_EPS) * fnw_ref[...]

    v = jnp.concatenate(v_blocks, axis=1)
    new_halo = ext_ref[tl:tl + halo, :]
    mu = jnp.mean(v, axis=-1, keepdims=True)
    cen = v - mu
    var = jnp.mean(cen * cen, axis=-1, keepdims=True)
    y = cen * lax.rsqrt(var + LN_EPS) * lnw_ref[...] + lnb_ref[...]
    y_conv = _dot(_silu(y).astype(BF16), wco_ref[...])
    merged = gc_ref[...].astype(F32) * y_conv + ysg_ref[...].astype(F32)
    x1 = x_ref[...] + gate1_ref[...] * _dot(merged.astype(BF16), wout_ref[...])
    h = _modulated_rms(x1, nw_ref[...], scale_ref[...], shift_ref[...])

    o_ref[...] = out
    ext_ref[0:halo, :] = new_halo
    x1_ref[...] = x1
    h_ref[...] = h.astype(BF16)


def _mixffn(u, gc, ysg, x, gate1, cw, cb, lnw, lnb, wco, wout,
            shift, scale, gate2, nw, fnw, win, wdn):
    bsz, seq, d = x.shape
    tl = MIXFFN_ROWS
    taps = cw.shape[0]
    halo = _round_up(taps - 1, SUBLANES)
    first = halo - (taps - 1)
    ph_rows = tl + max((first + k) // SUBLANES * SUBLANES
                       for k in range(taps) if (first + k) % SUBLANES)
    d_ff = wdn.shape[0]
    col_tiles = -(-d_ff // MXU_COLS)
    cuts = [min(-(-col_tiles * k // FFN_SLABS) * MXU_COLS, d_ff) for k in range(FFN_SLABS + 1)]
    slabs = tuple(zip(cuts[:-1], cuts[1:]))
    slab_w = max(s1 - s0 for s0, s1 in slabs)
    kern = functools.partial(_mixffn_kernel, halo=halo, slabs=slabs)
    n_tiles = seq // tl
    row = pl.BlockSpec((None, tl, d), lambda b, i: (b, jnp.minimum(i, n_tiles - 1), 0))
    out_row = pl.BlockSpec((None, tl, d), lambda b, i: (b, jnp.maximum(i - 1, 0), 0))
    vec = pl.BlockSpec((None, 1, d), lambda b, i: (b, 0, 0))
    r1 = _resident((1, d))
    return pl.pallas_call(
        kern,
        grid=(bsz, n_tiles + 1),
        in_specs=[row, row, row, row, vec,
                  _resident(cw.shape), r1, r1, r1, _resident(wco.shape), _resident(wout.shape),
                  vec, vec, vec, r1, r1, _resident(win.shape), _resident(wdn.shape)],
        out_specs=out_row,
        out_shape=jax.ShapeDtypeStruct((bsz, seq, d), F32),
        scratch_shapes=[pltpu.VMEM((tl + halo, d), F32),
                        pltpu.VMEM((2, SUBLANES - 1, ph_rows, LANES), F32),
                        pltpu.VMEM((tl, d), F32),
                        pltpu.VMEM((tl, d), BF16),
                        pltpu.VMEM((tl, slab_w), BF16)],
        compiler_params=_params(2),
        name="mixffn",
    )(u, gc, ysg, x, gate1, cw, cb, lnw, lnb, wco, wout, shift, scale, gate2, nw, fnw, win, wdn)


def _mods(c, w, b):
    bsz, d = c.shape
    mod = _ada(c, w, b)
    return [mod[:, i * d:(i + 1) * d].reshape(bsz, 1, d) for i in range(3)]


def _pad_lanes(v, width=LANES):
    return jnp.pad(v, [(0, 0)] * (v.ndim - 1) + [(0, width - v.shape[-1])])


def kernel(x, c, w_ada_mix, b_ada_mix, norm_mix_w, w_in, conv_dw_w, conv_dw_b, conv_ln_w, conv_ln_b, w_conv_out, ssm_conv_w, ssm_conv_b, dt_bias, a_log, d_skip, ssm_norm_w, w_ssm_out, w_out, w_ada_ffn, b_ada_ffn, norm_ffn_w, w_ffn_in, w_ffn_down, final_norm_w):
    bsz, seq, d = x.shape
    depth = w_in.shape[0]
    assert depth == 1, "the last kernel applies the final RMSNorm, so one layer only"
    d_conv = conv_dw_w.shape[-1]
    d_inner = ssm_norm_w.shape[-1]
    d_xbc = ssm_conv_w.shape[-1]
    n_heads = dt_bias.shape[-1]
    assert d_inner == n_heads * HEAD_DIM and d_xbc == d_inner + 2 * N_GROUPS * D_STATE
    assert n_heads <= LANES and seq % max(INPROJ_ROWS, SSD_ROWS, MIXFFN_ROWS) == 0
    row_vec = lambda v: v.reshape(1, -1).astype(F32)
    i = 0

    shift, scale, gate = _mods(c, w_ada_mix[i], b_ada_mix[i])
    w = w_in[i]
    s_dt = 2 * d_conv + d_inner + d_xbc
    w_lo = w[:, :s_dt].astype(BF16)
    w_hi = w[:, s_dt + n_heads:].astype(BF16)
    w_dt = _pad_lanes(w[:, s_dt:s_dt + n_heads]).astype(BF16)
    u, z, xbc, dt, gc, gs = _inproj(
        x, shift, scale, row_vec(norm_mix_w[i]), w_lo, w_hi, w_dt, _pad_lanes(row_vec(dt_bias[i])),
        ssm_conv_w[i], row_vec(ssm_conv_b[i]), d_conv=d_conv, d_inner=d_inner, d_xbc=d_xbc)
    dskip = row_vec(jnp.repeat(d_skip[i].astype(F32), HEAD_DIM))
    ysg = _ssd_branch(xbc, z, dt, gs, _pad_lanes(row_vec(a_log[i])), dskip,
                      row_vec(ssm_norm_w[i]), w_ssm_out[i].astype(BF16))
    shift2, scale2, gate2 = _mods(c, w_ada_ffn[i], b_ada_ffn[i])
    return _mixffn(u, gc, ysg, x, gate, conv_dw_w[i], row_vec(conv_dw_b[i]),
                   row_vec(conv_ln_w[i]), row_vec(conv_ln_b[i]), w_conv_out[i].astype(BF16),
                   w_out[i].astype(BF16), shift2, scale2, gate2, row_vec(norm_ffn_w[i]),
                   row_vec(final_norm_w), w_ffn_in[i].astype(BF16), w_ffn_down[i].astype(BF16))
```

```python
import functools
import math

import jax
import jax.numpy as jnp
from jax import lax
from jax.experimental import pallas as pl
from jax.experimental.pallas import tpu as pltpu

F32 = jnp.float32
BF16 = jnp.bfloat16

N_GROUPS = 4
D_STATE = 128
HEAD_DIM = 64
NORM_EPS = 1e-6
LN_EPS = 1e-5

LANES = 128
SUBLANES = 8
MXU_COLS = 256
SSD_CHUNK = LANES
HEADS_PER_BLOCK = LANES // HEAD_DIM
MASK_BIAS = -1e30
LOG2E = math.log2(math.e)
VMEM_LIMIT = 56 * 1024 * 1024

INPROJ_ROWS = 512
SSD_ROWS = 1024
MIXFFN_ROWS = 256
COL_TILE = 512
CONV_ROW_TILE = 64
FFN_SLABS = 2


def _sigmoid(v):
    return jax.nn.sigmoid(v)


def _silu(v):
    return v * jax.nn.sigmoid(v)


def _softplus(v):
    return jnp.maximum(v, 0.0) + jnp.log1p(jnp.exp(-jnp.abs(v)))


def _dot(a, b):
    return jnp.dot(a, b, preferred_element_type=F32)


def _resident(shape):
    nd = len(shape)
    return pl.BlockSpec(shape, lambda *_: (0,) * nd, pipeline_mode=pl.Buffered(1))


def _params(n_axes, flags=None):
    return pltpu.CompilerParams(
        dimension_semantics=("arbitrary",) * n_axes, vmem_limit_bytes=VMEM_LIMIT, flags=flags)


def _round_up(n, m):
    return -(-n // m) * m


def _token(*vals):
    tok = jnp.zeros((SUBLANES, LANES), F32)
    for val in vals:
        bits = lax.bitcast_convert_type(val[-SUBLANES:, -LANES:], jnp.int32)
        bits = lax.shift_right_logical(lax.shift_right_logical(bits, 16), 16)
        tok = tok + lax.bitcast_convert_type(bits, F32)
    return tok


def _wait_on(ref, cols, tok):
    packing = 4 // jnp.dtype(ref.dtype).itemsize
    region = ref[:, cols]
    reps = (region.shape[0] // (SUBLANES * packing), region.shape[1] // LANES)
    t = jnp.tile(jnp.concatenate([tok] * packing, axis=0).astype(ref.dtype), reps)
    ref[:, cols] = region + t


def _ada_kernel(c_ref, w_ref, b_ref, o_ref):
    c = c_ref[...]
    o_ref[...] = jnp.dot(_silu(c), w_ref[...], preferred_element_type=F32,
                         precision=lax.Precision.HIGHEST) + b_ref[...]


def _ada(c, w, b):
    bsz, d = c.shape
    n = w.shape[1]
    tn = d
    return pl.pallas_call(
        _ada_kernel,
        grid=(n // tn,),
        in_specs=[pl.BlockSpec((bsz, d), lambda j: (0, 0)),
                  pl.BlockSpec((d, tn), lambda j: (0, j)),
                  pl.BlockSpec((1, tn), lambda j: (0, j))],
        out_specs=pl.BlockSpec((bsz, tn), lambda j: (0, j)),
        out_shape=jax.ShapeDtypeStruct((bsz, n), F32),
        compiler_params=_params(1),
        name="ada",
    )(c, w, b.reshape(1, n))


def _modulated_rms(x, nw, scale, shift):
    ms = jnp.mean(x * x, axis=-1, keepdims=True)
    h = x * lax.rsqrt(ms + NORM_EPS) * nw
    return h * (1.0 + scale) + shift


def _inproj_kernel(x_ref, shift_ref, scale_ref, nw_ref, wlo_ref, whi_ref, wdt_ref, dtb_ref,
                   scw_ref, scb_ref,
                   u_ref, z_ref, xbc_ref, dt_ref, gc_ref, gs_ref,
                   h_ref, work_ref, halo_ref, *, d_conv, d_inner, d_xbc):
    tm = x_ref.shape[0]
    taps = scw_ref.shape[0]
    halo = halo_ref.shape[0]

    @pl.when(pl.program_id(1) == 0)
    def _():
        halo_ref[...] = jnp.zeros(halo_ref.shape, F32)

    h = _modulated_rms(x_ref[...], nw_ref[...], scale_ref[...], shift_ref[...])
    h_ref[...] = h.astype(BF16)

    def proj(c0):
        split = wlo_ref.shape[1]
        w = wlo_ref[:, c0:c0 + COL_TILE] if c0 < split else whi_ref[:, c0 - split:c0 - split + COL_TILE]
        return _dot(h_ref[...], w)

    d_model = gc_ref.shape[-1]
    z_off = 2 * d_conv
    xbc_off = z_off + d_inner
    gc_off = xbc_off + d_xbc
    gs_off = gc_off + d_model
    first = halo - (taps - 1)

    def glu_task(c0):
        a = proj(c0)
        b = proj(d_conv + c0)
        u_ref[:, c0:c0 + COL_TILE] = (a * _sigmoid(b)).astype(BF16)

    def act_task(out_ref, act, off, c0):
        out_ref[:, c0:c0 + COL_TILE] = act(proj(off + c0)).astype(BF16)

    def xbc_task(j):
        c0 = j * COL_TILE
        cs = slice(c0, c0 + COL_TILE)
        work = work_ref.at[j % 2]
        work[0:halo, :] = halo_ref[:, cs]
        work[halo:halo + tm, :] = proj(xbc_off + c0)
        acc = jnp.zeros((tm, COL_TILE), F32) + scb_ref[:, cs]
        for k in range(taps):
            acc = acc + scw_ref[k:k + 1, cs] * work[first + k:first + k + tm, :]
        xbc_ref[:, cs] = _silu(acc).astype(BF16)
        halo_ref[:, cs] = work[tm:tm + halo, :]

    light = [functools.partial(glu_task, j * COL_TILE) for j in range(d_conv // COL_TILE)]
    light += [functools.partial(act_task, z_ref, _silu, z_off, j * COL_TILE)
              for j in range(d_inner // COL_TILE)]
    light += [functools.partial(act_task, gc_ref, _sigmoid, gc_off, j * COL_TILE)
              for j in range(d_model // COL_TILE)]
    light += [functools.partial(act_task, gs_ref, _sigmoid, gs_off, j * COL_TILE)
              for j in range(d_model // COL_TILE)]
    heavy = [functools.partial(xbc_task, j) for j in range(d_xbc // COL_TILE)]
    per_heavy = -(-len(light) // len(heavy))
    for j, task in enumerate(heavy):
        task()
        for t in light[j * per_heavy:(j + 1) * per_heavy]:
            t()
    dt_raw = _dot(h_ref[...], wdt_ref[...]) + dtb_ref[...]
    dt_ref[...] = _softplus(dt_raw)


def _inproj(x, shift, scale, nw, w_lo, w_hi, w_dt, dt_bias, scw, scb, *, d_conv, d_inner, d_xbc):
    bsz, seq, d = x.shape
    tm = INPROJ_ROWS
    halo = _round_up(scw.shape[0] - 1, SUBLANES)
    kern = functools.partial(_inproj_kernel, d_conv=d_conv, d_inner=d_inner, d_xbc=d_xbc)
    row = lambda w: pl.BlockSpec((None, tm, w), lambda b, i: (b, i, 0))
    vec = pl.BlockSpec((None, 1, d), lambda b, i: (b, 0, 0))
    outs = [(d_conv, BF16), (d_inner, BF16), (d_xbc, BF16), (LANES, F32), (d, BF16), (d, BF16)]
    return pl.pallas_call(
        kern,
        grid=(bsz, seq // tm),
        in_specs=[row(d), vec, vec, _resident((1, d)), _resident(w_lo.shape), _resident(w_hi.shape),
                  _resident(w_dt.shape), _resident((1, LANES)),
                  _resident(scw.shape), _resident((1, d_xbc))],
        out_specs=[row(w) for w, _ in outs],
        out_shape=[jax.ShapeDtypeStruct((bsz, seq, w), t) for w, t in outs],
        scratch_shapes=[pltpu.VMEM((tm, d), BF16),
                        pltpu.VMEM((2, tm + halo, COL_TILE), F32),
                        pltpu.VMEM((halo, d_xbc), F32)],
        compiler_params=_params(2),
        name="inproj",
    )(x, shift, scale, nw, w_lo, w_hi, w_dt, dt_bias, scw, scb)


def _split3(v):
    hi = v.astype(BF16)
    r1 = v - hi.astype(F32)
    mid = r1.astype(BF16)
    lo = (r1 - mid.astype(F32)).astype(BF16)
    return hi, mid, lo


def _ssd_kernel(xbc_ref, z_ref, dt_ref, gs_ref, alog_ref, dskip_ref, snw_ref, wso_ref,
                o_ref, st_ref, y_ref, dec_ref):
    tl = xbc_ref.shape[0]
    d_inner = z_ref.shape[1]
    n_blocks = d_inner // LANES
    blocks_per_group = n_blocks // N_GROUPS
    b_off = d_inner
    c_off = d_inner + N_GROUPS * D_STATE
    ct = SSD_CHUNK

    @pl.when(pl.program_id(1) == 0)
    def _():
        st_ref[...] = jnp.zeros(st_ref.shape, F32)

    li = lax.broadcasted_iota(jnp.int32, (ct, ct), 0)
    si = lax.broadcasted_iota(jnp.int32, (ct, ct), 1)
    causal = li >= si
    tri = jnp.where(causal, 1.0, 0.0).astype(BF16)
    mask_bias = jnp.where(causal, 0.0, MASK_BIAS)
    low_half = si < HEAD_DIM

    for c in range(tl // ct):
        dtc = dt_ref[c * ct:(c + 1) * ct, :]
        da = dtc * -jnp.exp(alog_ref[...])
        hi, mid, lo = _split3(da)
        acs = _dot(tri, hi) + _dot(tri, mid) + _dot(tri, lo)
        acs2 = acs * LOG2E
        acs2_t = acs2.T
        dt_t = dtc.T
        dec_ref[c, 0] = acs2
        dec_ref[c, 1] = acs2_t - jnp.log2(dt_t)
        dec_ref[c, 2] = dt_t * jnp.exp2(acs2_t[:, ct - 1:ct] - acs2_t)

    def chunk(c, carry):
        r0 = pl.multiple_of(c * ct, ct)
        rows = pl.ds(r0, ct)
        acs2 = dec_ref[c, 0]
        row2_t = dec_ref[c, 1]
        w_t = dec_ref[c, 2]
        e_last = jnp.exp2(acs2[ct - 1:ct, :])
        for g in range(N_GROUPS):
            cg_b = xbc_ref[rows, c_off + g * D_STATE:c_off + (g + 1) * D_STATE]
            bg_b = xbc_ref[rows, b_off + g * D_STATE:b_off + (g + 1) * D_STATE]
            cb_mat = lax.dot_general(cg_b, bg_b, (((1,), (1,)), ((), ())),
                                     preferred_element_type=F32)
            bg_t = bg_b.astype(F32).T
            st_g = st_ref[g]
            y_off = _dot(cg_b, st_g.astype(BF16))
            new_cols = []
            for q in range(blocks_per_group):
                blk = g * blocks_per_group + q
                h0 = blk * HEADS_PER_BLOCK
                bs = slice(blk * LANES, (blk + 1) * LANES)
                xblk = xbc_ref[rows, bs]
                m_parts, b_parts, e_cols = [], [], []
                for hh in (h0, h0 + 1):
                    colb = jnp.broadcast_to(acs2[:, hh:hh + 1], (ct, ct))
                    m_parts.append(jnp.exp2(colb - row2_t[hh:hh + 1, :] + mask_bias) * cb_mat)
                    b_parts.append(bg_t * w_t[hh:hh + 1, :])
                    e_cols.append(jnp.exp2(colb))
                x_lo = jnp.where(low_half, xblk, jnp.zeros_like(xblk))
                x_hi = jnp.where(low_half, jnp.zeros_like(xblk), xblk)
                x_bd = jnp.concatenate([x_lo, x_hi], axis=0)
                m_cat = jnp.concatenate(m_parts, axis=1).astype(BF16)
                y_diag = _dot(m_cat, x_bd)
                e_blk = jnp.where(low_half, e_cols[0], e_cols[1])
                cs = slice(q * LANES, (q + 1) * LANES)
                y_ref[rows, bs] = (y_diag + e_blk * y_off[:, cs]
                                   + dskip_ref[:, bs] * xblk.astype(F32))
                b_cat = jnp.concatenate(b_parts, axis=1).astype(BF16)
                dec = jnp.where(low_half[0:1, :], e_last[:, h0:h0 + 1], e_last[:, h0 + 1:h0 + 2])
                new_cols.append(st_g[:, cs] * dec + _dot(b_cat, x_bd))
            st_ref[g] = jnp.concatenate(new_cols, axis=1)
        return carry

    lax.fori_loop(0, tl // ct, chunk, 0, unroll=2)

    gw = d_inner // N_GROUPS
    parts = []
    for g in range(N_GROUPS):
        gsl = slice(g * gw, (g + 1) * gw)
        yg = y_ref[:, gsl] * z_ref[:, gsl].astype(F32)
        ms = jnp.mean(yg * yg, axis=-1, keepdims=True)
        parts.append((yg * lax.rsqrt(ms + NORM_EPS) * snw_ref[:, gsl]).astype(BF16))
    yn = jnp.concatenate(parts, axis=1)
    o_ref[...] = (gs_ref[...].astype(F32) * _dot(yn, wso_ref[...])).astype(BF16)


def _ssd_branch(xbc, z, dt, gs, alog_pad, dskip, snw, wso):
    bsz, seq, d = gs.shape
    d_xbc = xbc.shape[-1]
    d_inner = z.shape[-1]
    tl = SSD_ROWS
    row = lambda w: pl.BlockSpec((None, tl, w), lambda b, i: (b, i, 0))
    return pl.pallas_call(
        _ssd_kernel,
        grid=(bsz, seq // tl),
        in_specs=[row(d_xbc), row(d_inner), row(LANES), row(d),
                  _resident((1, LANES)), _resident((1, d_inner)), _resident((1, d_inner)),
                  _resident(wso.shape)],
        out_specs=row(d),
        out_shape=jax.ShapeDtypeStruct((bsz, seq, d), BF16),
        scratch_shapes=[pltpu.VMEM((N_GROUPS, D_STATE, d_inner // N_GROUPS), F32),
                        pltpu.VMEM((tl, d_inner), F32),
                        pltpu.VMEM((tl // SSD_CHUNK, 3, SSD_CHUNK, LANES), F32)],
        compiler_params=_params(2),
        name="ssd",
    )(xbc, z, dt, gs, alog_pad, dskip, snw, wso)


def _mixffn_kernel(u_ref, gc_ref, ysg_ref, x_ref, gate1_ref,
                   cw_ref, cb_ref, lnw_ref, lnb_ref, wco_ref, wout_ref,
                   shift_ref, scale_ref, gate2_ref, nw_ref, fnw_ref, win_ref, wdn_ref,
                   o_ref,
                   ext_ref, ph_ref, x1_ref, h_ref, act_ref, *, halo, slabs):
    tl, d = u_ref.shape
    taps = cw_ref.shape[0]
    ph_rows = ph_ref.shape[2]
    step = pl.program_id(1)

    @pl.when(step == 0)
    def _():
        ext_ref[0:halo, :] = jnp.zeros((halo, d), F32)

    @pl.when((step == 0) & (pl.program_id(0) == 0))
    def _():
        x1_ref[...] = jnp.zeros(x1_ref.shape, F32)
        h_ref[...] = jnp.zeros(h_ref.shape, BF16)


    first = halo - (taps - 1)
    d_ff = wdn_ref.shape[0]

    ext_ref[halo:halo + tl, :] = u_ref[...].astype(F32)

    def conv_block(cb):
        cs = slice(cb * LANES, (cb + 1) * LANES)
        for p in range(1, SUBLANES):
            ph_ref[cb, p - 1] = ext_ref[p:p + ph_rows, cs]
        out = []
        for rb in range(tl // CONV_ROW_TILE):
            r0 = rb * CONV_ROW_TILE
            acc = jnp.zeros((CONV_ROW_TILE, LANES), F32) + cb_ref[:, cs]
            for k in range(taps):
                a8, p = divmod(first + k, SUBLANES)
                lo = r0 + a8 * SUBLANES
                if p == 0:
                    src = ext_ref[lo:lo + CONV_ROW_TILE, cs]
                else:
                    src = ph_ref[cb, p - 1, lo:lo + CONV_ROW_TILE, :]
                acc = acc + cw_ref[k:k + 1, cs] * src
            out.append(acc)
        return jnp.concatenate(out, axis=0)

    conv_blocks = list(range(d // LANES))
    n_stages = 3 * len(slabs)
    bounds = [-(-len(conv_blocks) * k // n_stages) for k in range(n_stages + 1)]
    shares = [conv_blocks[bounds[k]:bounds[k + 1]] for k in range(n_stages)]
    v_blocks = []
    tok = None
    ffn_acc = None

    def conv_stage(k, tok):
        outs = []
        for cb in shares[k]:
            if tok is not None:
                _wait_on(ext_ref, slice(cb * LANES, (cb + 1) * LANES), tok)
            outs.append(conv_block(cb))
        v_blocks.extend(outs)
        return outs[-1:]

    for si, (s0, s1) in enumerate(slabs):
        w = s1 - s0
        if tok is not None:
            _wait_on(h_ref, slice(None), tok)
        g = _dot(h_ref[...], win_ref[:, s0:s1])
        tok = _token(g, *conv_stage(3 * si, tok))

        up = _dot(h_ref[...], win_ref[:, d_ff + s0:d_ff + s1])
        act_ref[:, 0:w] = (_silu(g) * up).astype(BF16)
        tok = _token(up, *conv_stage(3 * si + 1, tok))

        part = _dot(act_ref[:, 0:w], wdn_ref[s0:s1, :])
        ffn_acc = part if ffn_acc is None else ffn_acc + part
        tok = _token(part, *conv_stage(3 * si + 2, tok))

    x2 = x1_ref[...] + gate2_ref[...] * ffn_acc
    ms = jnp.mean(x2 * x2, axis=-1, keepdims=True)
    out = x2 * lax.rsqrt(ms + NORM_EPS) * fnw_ref[...]

    v = jnp.concatenate(v_blocks, axis=1)
    new_halo = ext_ref[tl:tl + halo, :]
    mu = jnp.mean(v, axis=-1, keepdims=True)
    cen = v - mu
    var = jnp.mean(cen * cen, axis=-1, keepdims=True)
    y = cen * lax.rsqrt(var + LN_EPS) * lnw_ref[...] + lnb_ref[...]
    y_conv = _dot(_silu(y).astype(BF16), wco_ref[...])
    merged = gc_ref[...].astype(F32) * y_conv + ysg_ref[...].astype(F32)
    x1 = x_ref[...] + gate1_ref[...] * _dot(merged.astype(BF16), wout_ref[...])
    h = _modulated_rms(x1, nw_ref[...], scale_ref[...], shift_ref[...])

    o_ref[...] = out
    ext_ref[0:halo, :] = new_halo
    x1_ref[...] = x1
    h_ref[...] = h.astype(BF16)


def _mixffn(u, gc, ysg, x, gate1, cw, cb, lnw, lnb, wco, wout,
            shift, scale, gate2, nw, fnw, win, wdn):
    bsz, seq, d = x.shape
    tl = MIXFFN_ROWS
    taps = cw.shape[0]
    halo = _round_up(taps - 1, SUBLANES)
    first = halo - (taps - 1)
    ph_rows = tl + max((first + k) // SUBLANES * SUBLANES
                       for k in range(taps) if (first + k) % SUBLANES)
    d_ff = wdn.shape[0]
    col_tiles = -(-d_ff // MXU_COLS)
    cuts = [min(-(-col_tiles * k // FFN_SLABS) * MXU_COLS, d_ff) for k in range(FFN_SLABS + 1)]
    slabs = tuple(zip(cuts[:-1], cuts[1:]))
    slab_w = max(s1 - s0 for s0, s1 in slabs)
    kern = functools.partial(_mixffn_kernel, halo=halo, slabs=slabs)
    n_tiles = seq // tl
    row = pl.BlockSpec((None, tl, d), lambda b, i: (b, jnp.minimum(i, n_tiles - 1), 0))
    out_row = pl.BlockSpec((None, tl, d), lambda b, i: (b, jnp.maximum(i - 1, 0), 0))
    vec = pl.BlockSpec((None, 1, d), lambda b, i: (b, 0, 0))
    r1 = _resident((1, d))
    return pl.pallas_call(
        kern,
        grid=(bsz, n_tiles + 1),
        in_specs=[row, row, row, row, vec,
                  _resident(cw.shape), r1, r1, r1, _resident(wco.shape), _resident(wout.shape),
                  vec, vec, vec, r1, r1, _resident(win.shape), _resident(wdn.shape)],
        out_specs=out_row,
        out_shape=jax.ShapeDtypeStruct((bsz, seq, d), F32),
        scratch_shapes=[pltpu.VMEM((tl + halo, d), F32),
                        pltpu.VMEM((d // LANES, SUBLANES - 1, ph_rows, LANES), F32),
                        pltpu.VMEM((tl, d), F32),
                        pltpu.VMEM((tl, d), BF16),
                        pltpu.VMEM((tl, slab_w), BF16)],
        compiler_params=_params(2),
        name="mixffn",
    )(u, gc, ysg, x, gate1, cw, cb, lnw, lnb, wco, wout, shift, scale, gate2, nw, fnw, win, wdn)


def _mods(c, w, b):
    bsz, d = c.shape
    mod = _ada(c, w, b)
    return [mod[:, i * d:(i + 1) * d].reshape(bsz, 1, d) for i in range(3)]


def _pad_lanes(v, width=LANES):
    return jnp.pad(v, [(0, 0)] * (v.ndim - 1) + [(0, width - v.shape[-1])])


def kernel(x, c, w_ada_mix, b_ada_mix, norm_mix_w, w_in, conv_dw_w, conv_dw_b, conv_ln_w, conv_ln_b, w_conv_out, ssm_conv_w, ssm_conv_b, dt_bias, a_log, d_skip, ssm_norm_w, w_ssm_out, w_out, w_ada_ffn, b_ada_ffn, norm_ffn_w, w_ffn_in, w_ffn_down, final_norm_w):
    bsz, seq, d = x.shape
    depth = w_in.shape[0]
    assert depth == 1, "the last kernel applies the final RMSNorm, so one layer only"
    d_conv = conv_dw_w.shape[-1]
    d_inner = ssm_norm_w.shape[-1]
    d_xbc = ssm_conv_w.shape[-1]
    n_heads = dt_bias.shape[-1]
    assert d_inner == n_heads * HEAD_DIM and d_xbc == d_inner + 2 * N_GROUPS * D_STATE
    assert n_heads <= LANES and seq % max(INPROJ_ROWS, SSD_ROWS, MIXFFN_ROWS) == 0
    row_vec = lambda v: v.reshape(1, -1).astype(F32)
    i = 0

    shift, scale, gate = _mods(c, w_ada_mix[i], b_ada_mix[i])
    w = w_in[i]
    s_dt = 2 * d_conv + d_inner + d_xbc
    w_lo = w[:, :s_dt].astype(BF16)
    w_hi = w[:, s_dt + n_heads:].astype(BF16)
    w_dt = _pad_lanes(w[:, s_dt:s_dt + n_heads]).astype(BF16)
    u, z, xbc, dt, gc, gs = _inproj(
        x, shift, scale, row_vec(norm_mix_w[i]), w_lo, w_hi, w_dt, _pad_lanes(row_vec(dt_bias[i])),
        ssm_conv_w[i], row_vec(ssm_conv_b[i]), d_conv=d_conv, d_inner=d_inner, d_xbc=d_xbc)
    dskip = row_vec(jnp.repeat(d_skip[i].astype(F32), HEAD_DIM))
    ysg = _ssd_branch(xbc, z, dt, gs, _pad_lanes(row_vec(a_log[i])), dskip,
                      row_vec(ssm_norm_w[i]), w_ssm_out[i].astype(BF16))
    shift2, scale2, gate2 = _mods(c, w_ada_ffn[i], b_ada_ffn[i])
    return _mixffn(u, gc, ysg, x, gate, conv_dw_w[i], row_vec(conv_dw_b[i]),
                   row_vec(conv_ln_w[i]), row_vec(conv_ln_b[i]), w_conv_out[i].astype(BF16),
                   w_out[i].astype(BF16), shift2, scale2, gate2, row_vec(norm_ffn_w[i]),
                   row_vec(final_norm_w), w_ffn_in[i].astype(BF16), w_ffn_down[i].astype(BF16))
```

```python
import functools
import math

import jax
import jax.numpy as jnp
from jax import lax
from jax.experimental import pallas as pl
from jax.experimental.pallas import tpu as pltpu

F32 = jnp.float32
BF16 = jnp.bfloat16

N_GROUPS = 4
D_STATE = 128
HEAD_DIM = 64
NORM_EPS = 1e-6
LN_EPS = 1e-5

LANES = 128
SUBLANES = 8
MXU_COLS = 256
SSD_CHUNK = LANES
HEADS_PER_BLOCK = LANES // HEAD_DIM
MASK_BIAS = -1e30
LOG2E = math.log2(math.e)
VMEM_LIMIT = 56 * 1024 * 1024

INPROJ_ROWS = 512
SSD_ROWS = 1024
MIXFFN_ROWS = 256
COL_TILE = 512
CONV_ROW_TILE = 64
FFN_SLABS = 2


def _sigmoid(v):
    return jax.nn.sigmoid(v)


def _silu(v):
    return v * jax.nn.sigmoid(v)


def _softplus(v):
    return jnp.maximum(v, 0.0) + jnp.log1p(jnp.exp(-jnp.abs(v)))


def _dot(a, b):
    return jnp.dot(a, b, preferred_element_type=F32)


def _resident(shape):
    nd = len(shape)
    return pl.BlockSpec(shape, lambda *_: (0,) * nd, pipeline_mode=pl.Buffered(1))


def _params(n_axes, flags=None):
    return pltpu.CompilerParams(
        dimension_semantics=("arbitrary",) * n_axes, vmem_limit_bytes=VMEM_LIMIT, flags=flags)


def _round_up(n, m):
    return -(-n // m) * m


def _token(*vals):
    tok = jnp.zeros((SUBLANES, LANES), F32)
    for val in vals:
        bits = lax.bitcast_convert_type(val[-SUBLANES:, -LANES:], jnp.int32)
        bits = lax.shift_right_logical(lax.shift_right_logical(bits, 16), 16)
        tok = tok + lax.bitcast_convert_type(bits, F32)
    return tok


def _wait_on(ref, cols, tok):
    packing = 4 // jnp.dtype(ref.dtype).itemsize
    region = ref[:, cols]
    reps = (region.shape[0] // (SUBLANES * packing), region.shape[1] // LANES)
    t = jnp.tile(jnp.concatenate([tok] * packing, axis=0).astype(ref.dtype), reps)
    ref[:, cols] = region + t


def _ada_kernel(c_ref, w_ref, b_ref, o_ref):
    c = c_ref[...]
    o_ref[...] = jnp.dot(_silu(c), w_ref[...], preferred_element_type=F32,
                         precision=lax.Precision.HIGHEST) + b_ref[...]


def _ada(c, w, b):
    bsz, d = c.shape
    n = w.shape[1]
    tn = d
    return pl.pallas_call(
        _ada_kernel,
        grid=(n // tn,),
        in_specs=[pl.BlockSpec((bsz, d), lambda j: (0, 0)),
                  pl.BlockSpec((d, tn), lambda j: (0, j)),
                  pl.BlockSpec((1, tn), lambda j: (0, j))],
        out_specs=pl.BlockSpec((bsz, tn), lambda j: (0, j)),
        out_shape=jax.ShapeDtypeStruct((bsz, n), F32),
        compiler_params=_params(1),
        name="ada",
    )(c, w, b.reshape(1, n))


def _modulated_rms(x, nw, scale, shift):
    ms = jnp.mean(x * x, axis=-1, keepdims=True)
    h = x * lax.rsqrt(ms + NORM_EPS) * nw
    return h * (1.0 + scale) + shift


def _inproj_kernel(x_ref, shift_ref, scale_ref, nw_ref, wlo_ref, whi_ref, wdt_ref, dtb_ref,
                   scw_ref, scb_ref,
                   u_ref, z_ref, xbc_ref, dt_ref, gc_ref, gs_ref,
                   h_ref, work_ref, halo_ref, *, d_conv, d_inner, d_xbc):
    tm = x_ref.shape[0]
    taps = scw_ref.shape[0]
    halo = halo_ref.shape[0]

    @pl.when(pl.program_id(1) == 0)
    def _():
        halo_ref[...] = jnp.zeros(halo_ref.shape, F32)

    h = _modulated_rms(x_ref[...], nw_ref[...], scale_ref[...], shift_ref[...])
    h_ref[...] = h.astype(BF16)

    def proj(c0):
        split = 2 * d_conv + d_inner + d_xbc
        w = wlo_ref[:, c0:c0 + COL_TILE] if c0 < split else whi_ref[:, c0 - split:c0 - split + COL_TILE]
        return _dot(h_ref[...], w)

    d_model = gc_ref.shape[-1]
    z_off = 2 * d_conv
    xbc_off = z_off + d_inner
    gc_off = xbc_off + d_xbc
    gs_off = gc_off + d_model
    first = halo - (taps - 1)

    def glu_task(c0):
        a = proj(c0)
        b = proj(d_conv + c0)
        u_ref[:, c0:c0 + COL_TILE] = (a * _sigmoid(b)).astype(BF16)

    def act_task(out_ref, act, off, c0):
        out_ref[:, c0:c0 + COL_TILE] = act(proj(off + c0)).astype(BF16)

    def xbc_task(j):
        c0 = j * COL_TILE
        cs = slice(c0, c0 + COL_TILE)
        work = work_ref.at[j % 2]
        work[0:halo, :] = halo_ref[:, cs]
        work[halo:halo + tm, :] = proj(xbc_off + c0)
        acc = jnp.zeros((tm, COL_TILE), F32) + scb_ref[:, cs]
        for k in range(taps):
            acc = acc + scw_ref[k:k + 1, cs] * work[first + k:first + k + tm, :]
        xbc_ref[:, cs] = _silu(acc).astype(BF16)
        halo_ref[:, cs] = work[tm:tm + halo, :]

    light = [functools.partial(glu_task, j * COL_TILE) for j in range(d_conv // COL_TILE)]
    light += [functools.partial(act_task, z_ref, _silu, z_off, j * COL_TILE)
              for j in range(d_inner // COL_TILE)]
    light += [functools.partial(act_task, gc_ref, _sigmoid, gc_off, j * COL_TILE)
              for j in range(d_model // COL_TILE)]
    light += [functools.partial(act_task, gs_ref, _sigmoid, gs_off, j * COL_TILE)
              for j in range(d_model // COL_TILE)]
    heavy = [functools.partial(xbc_task, j) for j in range(d_xbc // COL_TILE)]
    per_heavy = -(-len(light) // len(heavy))
    for j, task in enumerate(heavy):
        task()
        for t in light[j * per_heavy:(j + 1) * per_heavy]:
            t()
    dt_raw = _dot(h_ref[...], wdt_ref[...]) + dtb_ref[...]
    dt_ref[...] = _softplus(dt_raw)


def _inproj(x, shift, scale, nw, w_lo, w_hi, w_dt, dt_bias, scw, scb, *, d_conv, d_inner, d_xbc):
    bsz, seq, d = x.shape
    tm = INPROJ_ROWS
    halo = _round_up(scw.shape[0] - 1, SUBLANES)
    kern = functools.partial(_inproj_kernel, d_conv=d_conv, d_inner=d_inner, d_xbc=d_xbc)
    row = lambda w: pl.BlockSpec((None, tm, w), lambda b, i: (b, i, 0))
    vec = pl.BlockSpec((None, 1, d), lambda b, i: (b, 0, 0))
    outs = [(d_conv, BF16), (d_inner, BF16), (d_xbc, BF16), (LANES, F32), (d, BF16), (d, BF16)]
    return pl.pallas_call(
        kern,
        grid=(bsz, seq // tm),
        in_specs=[row(d), vec, vec, _resident((1, d)), _resident(w_lo.shape), _resident(w_hi.shape),
                  _resident(w_dt.shape), _resident((1, LANES)),
                  _resident(scw.shape), _resident((1, d_xbc))],
        out_specs=[row(w) for w, _ in outs],
        out_shape=[jax.ShapeDtypeStruct((bsz, seq, w), t) for w, t in outs],
        scratch_shapes=[pltpu.VMEM((tm, d), BF16),
                        pltpu.VMEM((2, tm + halo, COL_TILE), F32),
                        pltpu.VMEM((halo, d_xbc), F32)],
        compiler_params=_params(2),
        name="inproj",
    )(x, shift, scale, nw, w_lo, w_hi, w_dt, dt_bias, scw, scb)


def _split3(v):
    hi = v.astype(BF16)
    r1 = v - hi.astype(F32)
    mid = r1.astype(BF16)
    lo = (r1 - mid.astype(F32)).astype(BF16)
    return hi, mid, lo


def _ssd_kernel(xbc_ref, z_ref, dt_ref, gs_ref, alog_ref, dskip_ref, snw_ref, wso_ref,
                o_ref, st_ref, y_ref, dec_ref):
    tl = xbc_ref.shape[0]
    d_inner = z_ref.shape[1]
    n_blocks = d_inner // LANES
    blocks_per_group = n_blocks // N_GROUPS
    b_off = d_inner
    c_off = d_inner + N_GROUPS * D_STATE
    ct = SSD_CHUNK

    @pl.when(pl.program_id(1) == 0)
    def _():
        st_ref[...] = jnp.zeros(st_ref.shape, F32)

    li = lax.broadcasted_iota(jnp.int32, (ct, ct), 0)
    si = lax.broadcasted_iota(jnp.int32, (ct, ct), 1)
    causal = li >= si
    tri = jnp.where(causal, 1.0, 0.0).astype(BF16)
    mask_bias = jnp.where(causal, 0.0, MASK_BIAS)
    low_half = si < HEAD_DIM

    for c in range(tl // ct):
        dtc = dt_ref[c * ct:(c + 1) * ct, :]
        da = dtc * -jnp.exp(alog_ref[...])
        hi, mid, lo = _split3(da)
        acs = _dot(tri, hi) + _dot(tri, mid) + _dot(tri, lo)
        acs2 = acs * LOG2E
        acs2_t = acs2.T
        dt_t = dtc.T
        dec_ref[c, 0] = acs2
        dec_ref[c, 1] = acs2_t - jnp.log2(dt_t)
        dec_ref[c, 2] = dt_t * jnp.exp2(acs2_t[:, ct - 1:ct] - acs2_t)

    def chunk(c, carry):
        r0 = pl.multiple_of(c * ct, ct)
        rows = pl.ds(r0, ct)
        acs2 = dec_ref[c, 0]
        row2_t = dec_ref[c, 1]
        w_t = dec_ref[c, 2]
        e_last = jnp.exp2(acs2[ct - 1:ct, :])
        for g in range(N_GROUPS):
            cg_b = xbc_ref[rows, c_off + g * D_STATE:c_off + (g + 1) * D_STATE]
            bg_b = xbc_ref[rows, b_off + g * D_STATE:b_off + (g + 1) * D_STATE]
            cb_mat = lax.dot_general(cg_b, bg_b, (((1,), (1,)), ((), ())),
                                     preferred_element_type=F32)
            bg_t = bg_b.astype(F32).T
            st_g = st_ref[g]
            y_off = _dot(cg_b, st_g.astype(BF16))
            new_cols = []
            for q in range(blocks_per_group):
                blk = g * blocks_per_group + q
                h0 = blk * HEADS_PER_BLOCK
                bs = slice(blk * LANES, (blk + 1) * LANES)
                xblk = xbc_ref[rows, bs]
                m_parts, b_parts, e_cols = [], [], []
                for hh in (h0, h0 + 1):
                    colb = jnp.broadcast_to(acs2[:, hh:hh + 1], (ct, ct))
                    m_parts.append(jnp.exp2(colb - row2_t[hh:hh + 1, :] + mask_bias) * cb_mat)
                    b_parts.append(bg_t * w_t[hh:hh + 1, :])
                    e_cols.append(jnp.exp2(colb))
                x_lo = jnp.where(low_half, xblk, jnp.zeros_like(xblk))
                x_hi = jnp.where(low_half, jnp.zeros_like(xblk), xblk)
                x_bd = jnp.concatenate([x_lo, x_hi], axis=0)
                m_cat = jnp.concatenate(m_parts, axis=1).astype(BF16)
                y_diag = _dot(m_cat, x_bd)
                e_blk = jnp.where(low_half, e_cols[0], e_cols[1])
                cs = slice(q * LANES, (q + 1) * LANES)
                y_ref[rows, bs] = (y_diag + e_blk * y_off[:, cs]
                                   + dskip_ref[:, bs] * xblk.astype(F32))
                b_cat = jnp.concatenate(b_parts, axis=1).astype(BF16)
                dec = jnp.where(low_half[0:1, :], e_last[:, h0:h0 + 1], e_last[:, h0 + 1:h0 + 2])
                new_cols.append(st_g[:, cs] * dec + _dot(b_cat, x_bd))
            st_ref[g] = jnp.concatenate(new_cols, axis=1)
        return carry

    lax.fori_loop(0, tl // ct, chunk, 0, unroll=2)

    gw = d_inner // N_GROUPS
    parts = []
    for g in range(N_GROUPS):
        gsl = slice(g * gw, (g + 1) * gw)
        yg = y_ref[:, gsl] * z_ref[:, gsl].astype(F32)
        ms = jnp.mean(yg * yg, axis=-1, keepdims=True)
        parts.append((yg * lax.rsqrt(ms + NORM_EPS) * snw_ref[:, gsl]).astype(BF16))
    yn = jnp.concatenate(parts, axis=1)
    o_ref[...] = (gs_ref[...].astype(F32) * _dot(yn, wso_ref[...])).astype(BF16)


def _ssd_branch(xbc, z, dt, gs, alog_pad, dskip, snw, wso):
    bsz, seq, d = gs.shape
    d_xbc = xbc.shape[-1]
    d_inner = z.shape[-1]
    tl = SSD_ROWS
    row = lambda w: pl.BlockSpec((None, tl, w), lambda b, i: (b, i, 0))
    return pl.pallas_call(
        _ssd_kernel,
        grid=(bsz, seq // tl),
        in_specs=[row(d_xbc), row(d_inner), row(LANES), row(d),
                  _resident((1, LANES)), _resident((1, d_inner)), _resident((1, d_inner)),
                  _resident(wso.shape)],
        out_specs=row(d),
        out_shape=jax.ShapeDtypeStruct((bsz, seq, d), BF16),
        scratch_shapes=[pltpu.VMEM((N_GROUPS, D_STATE, d_inner // N_GROUPS), F32),
                        pltpu.VMEM((tl, d_inner), F32),
                        pltpu.VMEM((tl // SSD_CHUNK, 3, SSD_CHUNK, LANES), F32)],
        compiler_params=_params(2),
        name="ssd",
    )(xbc, z, dt, gs, alog_pad, dskip, snw, wso)


def _mixffn_kernel(u_ref, gc_ref, ysg_ref, x_ref, gate1_ref,
                   cw_ref, cb_ref, lnw_ref, lnb_ref, wco_ref, wout_ref,
                   shift_ref, scale_ref, gate2_ref, nw_ref, fnw_ref, win_ref, wdn_ref,
                   o_ref,
                   ext_ref, ph_ref, x1_ref, h_ref, act_ref, *, halo, slabs):
    tl, d = u_ref.shape
    taps = cw_ref.shape[0]
    ph_rows = ph_ref.shape[2]
    step = pl.program_id(1)

    @pl.when(step == 0)
    def _():
        ext_ref[0:halo, :] = jnp.zeros((halo, d), F32)

    @pl.when((step == 0) & (pl.program_id(0) == 0))
    def _():
        x1_ref[...] = jnp.zeros(x1_ref.shape, F32)
        h_ref[...] = jnp.zeros(h_ref.shape, BF16)


    first = halo - (taps - 1)
    d_ff = wdn_ref.shape[0]

    ext_ref[halo:halo + tl, :] = u_ref[...].astype(F32)

    def conv_block(cb):
        cs = slice(cb * LANES, (cb + 1) * LANES)
        for p in range(1, SUBLANES):
            ph_ref[cb, p - 1] = ext_ref[p:p + ph_rows, cs]
        out = []
        for rb in range(tl // CONV_ROW_TILE):
            r0 = rb * CONV_ROW_TILE
            acc = jnp.zeros((CONV_ROW_TILE, LANES), F32) + cb_ref[:, cs]
            for k in range(taps):
                a8, p = divmod(first + k, SUBLANES)
                lo = r0 + a8 * SUBLANES
                if p == 0:
                    src = ext_ref[lo:lo + CONV_ROW_TILE, cs]
                else:
                    src = ph_ref[cb, p - 1, lo:lo + CONV_ROW_TILE, :]
                acc = acc + cw_ref[k:k + 1, cs] * src
            out.append(acc)
        return jnp.concatenate(out, axis=0)

    conv_blocks = list(range(d // LANES))
    n_stages = 3 * len(slabs)
    bounds = [-(-len(conv_blocks) * k // n_stages) for k in range(n_stages + 1)]
    shares = [conv_blocks[bounds[k]:bounds[k + 1]] for k in range(n_stages)]
    v_blocks = []
    tok = None
    ffn_acc = None

    def conv_stage(k, tok):
        outs = []
        for cb in shares[k]:
            if tok is not None:
                _wait_on(ext_ref, slice(cb * LANES, (cb + 1) * LANES), tok)
            outs.append(conv_block(cb))
        v_blocks.extend(outs)
        return outs[-1:]

    for si, (s0, s1) in enumerate(slabs):
        w = s1 - s0
        if tok is not None:
            _wait_on(h_ref, slice(None), tok)
        g = _dot(h_ref[...], win_ref[:, s0:s1])
        tok = _token(g, *conv_stage(3 * si, tok))

        up = _dot(h_ref[...], win_ref[:, d_ff + s0:d_ff + s1])
        act_ref[:, 0:w] = (_silu(g) * up).astype(BF16)
        tok = _token(up, *conv_stage(3 * si + 1, tok))

        part = _dot(act_ref[:, 0:w], wdn_ref[s0:s1, :])
        ffn_acc = part if ffn_acc is None else ffn_acc + part
        tok = _token(part, *conv_stage(3 * si + 2, tok))

    x2 = x1_ref[...] + gate2_ref[...] * ffn_acc
    ms = jnp.mean(x2 * x2, axis=-1, keepdims=True)
    out = x2 * lax.rsqrt(ms + NORM_EPS) * fnw_ref[...]

    v = jnp.concatenate(v_blocks, axis=1)
    new_halo = ext_ref[tl:tl + halo, :]
    mu = jnp.mean(v, axis=-1, keepdims=True)
    cen = v - mu
    var = jnp.mean(cen * cen, axis=-1, keepdims=True)
    y = cen * lax.rsqrt(var + LN_EPS) * lnw_ref[...] + lnb_ref[...]
    y_conv = _dot(_silu(y).astype(BF16), wco_ref[...])
    merged = gc_ref[...].astype(F32) * y_conv + ysg_ref[...].astype(F32)
    x1 = x_ref[...] + gate1_ref[...] * _dot(merged.astype(BF16), wout_ref[...])
    h = _modulated_rms(x1, nw_ref[...], scale_ref[...], shift_ref[...])

    o_ref[...] = out
    ext_ref[0:halo, :] = new_halo
    x1_ref[...] = x1
    h_ref[...] = h.astype(BF16)


def _mixffn(u, gc, ysg, x, gate1, cw, cb, lnw, lnb, wco, wout,
            shift, scale, gate2, nw, fnw, win, wdn):
    bsz, seq, d = x.shape
    tl = MIXFFN_ROWS
    taps = cw.shape[0]
    halo = _round_up(taps - 1, SUBLANES)
    first = halo - (taps - 1)
    ph_rows = tl + max((first + k) // SUBLANES * SUBLANES
                       for k in range(taps) if (first + k) % SUBLANES)
    d_ff = wdn.shape[0]
    col_tiles = -(-d_ff // MXU_COLS)
    cuts = [min(-(-col_tiles * k // FFN_SLABS) * MXU_COLS, d_ff) for k in range(FFN_SLABS + 1)]
    slabs = tuple(zip(cuts[:-1], cuts[1:]))
    slab_w = max(s1 - s0 for s0, s1 in slabs)
    kern = functools.partial(_mixffn_kernel, halo=halo, slabs=slabs)
    n_tiles = seq // tl
    row = pl.BlockSpec((None, tl, d), lambda b, i: (b, jnp.minimum(i, n_tiles - 1), 0))
    out_row = pl.BlockSpec((None, tl, d), lambda b, i: (b, jnp.maximum(i - 1, 0), 0))
    vec = pl.BlockSpec((None, 1, d), lambda b, i: (b, 0, 0))
    r1 = _resident((1, d))
    return pl.pallas_call(
        kern,
        grid=(bsz, n_tiles + 1),
        in_specs=[row, row, row, row, vec,
                  _resident(cw.shape), r1, r1, r1, _resident(wco.shape), _resident(wout.shape),
                  vec, vec, vec, r1, r1, _resident(win.shape), _resident(wdn.shape)],
        out_specs=out_row,
        out_shape=jax.ShapeDtypeStruct((bsz, seq, d), F32),
        scratch_shapes=[pltpu.VMEM((tl + halo, d), F32),
                        pltpu.VMEM((d // LANES, SUBLANES - 1, ph_rows, LANES), F32),
                        pltpu.VMEM((tl, d), F32),
                        pltpu.VMEM((tl, d), BF16),
                        pltpu.VMEM((tl, slab_w), BF16)],
        compiler_params=_params(2),
        name="mixffn",
    )(u, gc, ysg, x, gate1, cw, cb, lnw, lnb, wco, wout, shift, scale, gate2, nw, fnw, win, wdn)


def _mods(c, w, b):
    bsz, d = c.shape
    mod = _ada(c, w, b)
    return [mod[:, i * d:(i + 1) * d].reshape(bsz, 1, d) for i in range(3)]


def _pad_lanes(v, width=LANES):
    return jnp.pad(v, [(0, 0)] * (v.ndim - 1) + [(0, width - v.shape[-1])])


def kernel(x, c, w_ada_mix, b_ada_mix, norm_mix_w, w_in, conv_dw_w, conv_dw_b, conv_ln_w, conv_ln_b, w_conv_out, ssm_conv_w, ssm_conv_b, dt_bias, a_log, d_skip, ssm_norm_w, w_ssm_out, w_out, w_ada_ffn, b_ada_ffn, norm_ffn_w, w_ffn_in, w_ffn_down, final_norm_w):
    bsz, seq, d = x.shape
    depth = w_in.shape[0]
    assert depth == 1, "the last kernel applies the final RMSNorm, so one layer only"
    d_conv = conv_dw_w.shape[-1]
    d_inner = ssm_norm_w.shape[-1]
    d_xbc = ssm_conv_w.shape[-1]
    n_heads = dt_bias.shape[-1]
    assert d_inner == n_heads * HEAD_DIM and d_xbc == d_inner + 2 * N_GROUPS * D_STATE
    assert n_heads <= LANES and seq % max(INPROJ_ROWS, SSD_ROWS, MIXFFN_ROWS) == 0
    row_vec = lambda v: v.reshape(1, -1).astype(F32)
    i = 0

    shift, scale, gate = _mods(c, w_ada_mix[i], b_ada_mix[i])
    w = w_in[i]
    s_dt = 2 * d_conv + d_inner + d_xbc
    w_lo = w.astype(BF16)
    w_hi = w[:, s_dt + n_heads:].astype(BF16)
    w_dt = _pad_lanes(w[:, s_dt:s_dt + n_heads]).astype(BF16)
    u, z, xbc, dt, gc, gs = _inproj(
        x, shift, scale, row_vec(norm_mix_w[i]), w_lo, w_hi, w_dt, _pad_lanes(row_vec(dt_bias[i])),
        ssm_conv_w[i], row_vec(ssm_conv_b[i]), d_conv=d_conv, d_inner=d_inner, d_xbc=d_xbc)
    dskip = row_vec(jnp.repeat(d_skip[i].astype(F32), HEAD_DIM))
    ysg = _ssd_branch(xbc, z, dt, gs, _pad_lanes(row_vec(a_log[i])), dskip,
                      row_vec(ssm_norm_w[i]), w_ssm_out[i].astype(BF16))
    shift2, scale2, gate2 = _mods(c, w_ada_ffn[i], b_ada_ffn[i])
    return _mixffn(u, gc, ysg, x, gate, conv_dw_w[i], row_vec(conv_dw_b[i]),
                   row_vec(conv_ln_w[i]), row_vec(conv_ln_b[i]), w_conv_out[i].astype(BF16),
                   w_out[i].astype(BF16), shift2, scale2, gate2, row_vec(norm_ffn_w[i]),
                   row_vec(final_norm_w), w_ffn_in[i].astype(BF16), w_ffn_down[i].astype(BF16))
```

```python
import functools
import math

import jax
import jax.numpy as jnp
from jax import lax
from jax.experimental import pallas as pl
from jax.experimental.pallas import tpu as pltpu

F32 = jnp.float32
BF16 = jnp.bfloat16

N_GROUPS = 4
D_STATE = 128
HEAD_DIM = 64
NORM_EPS = 1e-6
LN_EPS = 1e-5

LANES = 128
SUBLANES = 8
MXU_COLS = 256
SSD_CHUNK = LANES
HEADS_PER_BLOCK = LANES // HEAD_DIM
MASK_BIAS = -1e30
LOG2E = math.log2(math.e)
VMEM_LIMIT = 56 * 1024 * 1024

INPROJ_ROWS = 512
SSD_ROWS = 1024
MIXFFN_ROWS = 256
COL_TILE = 512
CONV_ROW_TILE = 64
FFN_SLABS = 2


def _sigmoid(v):
    return jax.nn.sigmoid(v)


def _silu(v):
    return v * jax.nn.sigmoid(v)


def _softplus(v):
    return jnp.maximum(v, 0.0) + jnp.log1p(jnp.exp(-jnp.abs(v)))


def _dot(a, b):
    return jnp.dot(a, b, preferred_element_type=F32)


def _resident(shape):
    nd = len(shape)
    return pl.BlockSpec(shape, lambda *_: (0,) * nd, pipeline_mode=pl.Buffered(1))


def _params(n_axes, flags=None):
    return pltpu.CompilerParams(
        dimension_semantics=("arbitrary",) * n_axes, vmem_limit_bytes=VMEM_LIMIT, flags=flags)


def _round_up(n, m):
    return -(-n // m) * m


def _token(*vals):
    tok = jnp.zeros((SUBLANES, LANES), F32)
    for val in vals:
        bits = lax.bitcast_convert_type(val[-SUBLANES:, -LANES:], jnp.int32)
        bits = lax.shift_right_logical(lax.shift_right_logical(bits, 16), 16)
        tok = tok + lax.bitcast_convert_type(bits, F32)
    return tok


def _wait_on(ref, cols, tok):
    packing = 4 // jnp.dtype(ref.dtype).itemsize
    region = ref[:, cols]
    reps = (region.shape[0] // (SUBLANES * packing), region.shape[1] // LANES)
    t = jnp.tile(jnp.concatenate([tok] * packing, axis=0).astype(ref.dtype), reps)
    ref[:, cols] = region + t


def _ada_kernel(c_ref, w_ref, b_ref, o_ref):
    c = c_ref[...]
    o_ref[...] = jnp.dot(_silu(c), w_ref[...], preferred_element_type=F32,
                         precision=lax.Precision.HIGHEST) + b_ref[...]


def _ada(c, w, b):
    bsz, d = c.shape
    n = w.shape[1]
    tn = d
    return pl.pallas_call(
        _ada_kernel,
        grid=(n // tn,),
        in_specs=[pl.BlockSpec((bsz, d), lambda j: (0, 0)),
                  pl.BlockSpec((d, tn), lambda j: (0, j)),
                  pl.BlockSpec((1, tn), lambda j: (0, j))],
        out_specs=pl.BlockSpec((bsz, tn), lambda j: (0, j)),
        out_shape=jax.ShapeDtypeStruct((bsz, n), F32),
        compiler_params=_params(1),
        name="ada",
    )(c, w, b.reshape(1, n))


def _modulated_rms(x, nw, scale, shift):
    ms = jnp.mean(x * x, axis=-1, keepdims=True)
    h = x * lax.rsqrt(ms + NORM_EPS) * nw
    return h * (1.0 + scale) + shift


def _inproj_kernel(x_ref, shift_ref, scale_ref, nw_ref, wlo_ref, whi_ref, wdt_ref, dtb_ref,
                   scw_ref, scb_ref,
                   u_ref, z_ref, xbc_ref, dt_ref, gc_ref, gs_ref,
                   h_ref, work_ref, halo_ref, *, d_conv, d_inner, d_xbc):
    tm = x_ref.shape[0]
    taps = scw_ref.shape[0]
    halo = halo_ref.shape[0]

    @pl.when(pl.program_id(1) == 0)
    def _():
        halo_ref[...] = jnp.zeros(halo_ref.shape, F32)

    h = _modulated_rms(x_ref[...], nw_ref[...], scale_ref[...], shift_ref[...])
    h_ref[...] = h.astype(BF16)

    def proj(c0):
        split = 2 * d_conv + d_inner + d_xbc
        w = wlo_ref[:, c0:c0 + COL_TILE] if c0 < split else whi_ref[:, c0 - split:c0 - split + COL_TILE]
        return _dot(h_ref[...], w)

    d_model = gc_ref.shape[-1]
    z_off = 2 * d_conv
    xbc_off = z_off + d_inner
    gc_off = xbc_off + d_xbc
    gs_off = gc_off + d_model
    first = halo - (taps - 1)

    def glu_task(c0):
        a = proj(c0)
        b = proj(d_conv + c0)
        u_ref[:, c0:c0 + COL_TILE] = (a * _sigmoid(b)).astype(BF16)

    def act_task(out_ref, act, off, c0):
        out_ref[:, c0:c0 + COL_TILE] = act(proj(off + c0)).astype(BF16)

    def xbc_task(j):
        c0 = j * COL_TILE
        cs = slice(c0, c0 + COL_TILE)
        work = work_ref.at[j % 2]
        work[0:halo, :] = halo_ref[:, cs]
        work[halo:halo + tm, :] = proj(xbc_off + c0)
        acc = jnp.zeros((tm, COL_TILE), F32) + scb_ref[:, cs]
        for k in range(taps):
            acc = acc + scw_ref[k:k + 1, cs] * work[first + k:first + k + tm, :]
        xbc_ref[:, cs] = _silu(acc).astype(BF16)
        halo_ref[:, cs] = work[tm:tm + halo, :]

    light = [functools.partial(glu_task, j * COL_TILE) for j in range(d_conv // COL_TILE)]
    light += [functools.partial(act_task, z_ref, _silu, z_off, j * COL_TILE)
              for j in range(d_inner // COL_TILE)]
    light += [functools.partial(act_task, gc_ref, _sigmoid, gc_off, j * COL_TILE)
              for j in range(d_model // COL_TILE)]
    light += [functools.partial(act_task, gs_ref, _sigmoid, gs_off, j * COL_TILE)
              for j in range(d_model // COL_TILE)]
    heavy = [functools.partial(xbc_task, j) for j in range(d_xbc // COL_TILE)]
    per_heavy = -(-len(light) // len(heavy))
    for j, task in enumerate(heavy):
        task()
        for t in light[j * per_heavy:(j + 1) * per_heavy]:
            t()
    dt_raw = _dot(h_ref[...], wdt_ref[...]) + dtb_ref[...]
    dt_ref[...] = _softplus(dt_raw)


def _inproj(x, shift, scale, nw, w_lo, w_hi, w_dt, dt_bias, scw, scb, *, d_conv, d_inner, d_xbc):
    bsz, seq, d = x.shape
    tm = INPROJ_ROWS
    halo = _round_up(scw.shape[0] - 1, SUBLANES)
    kern = functools.partial(_inproj_kernel, d_conv=d_conv, d_inner=d_inner, d_xbc=d_xbc)
    row = lambda w: pl.BlockSpec((None, tm, w), lambda b, i: (b, i, 0))
    vec = pl.BlockSpec((None, 1, d), lambda b, i: (b, 0, 0))
    outs = [(d_conv, BF16), (d_inner, BF16), (d_xbc, BF16), (LANES, F32), (d, BF16), (d, BF16)]
    return pl.pallas_call(
        kern,
        grid=(bsz, seq // tm),
        in_specs=[row(d), vec, vec, _resident((1, d)), _resident(w_lo.shape), _resident(w_hi.shape),
                  _resident(w_dt.shape), _resident((1, LANES)),
                  _resident(scw.shape), _resident((1, d_xbc))],
        out_specs=[row(w) for w, _ in outs],
        out_shape=[jax.ShapeDtypeStruct((bsz, seq, w), t) for w, t in outs],
        scratch_shapes=[pltpu.VMEM((tm, d), BF16),
                        pltpu.VMEM((2, tm + halo, COL_TILE), F32),
                        pltpu.VMEM((halo, d_xbc), F32)],
        compiler_params=_params(2),
        name="inproj",
    )(x, shift, scale, nw, w_lo, w_hi, w_dt, dt_bias, scw, scb)


def _split3(v):
    hi = v.astype(BF16)
    r1 = v - hi.astype(F32)
    mid = r1.astype(BF16)
    lo = (r1 - mid.astype(F32)).astype(BF16)
    return hi, mid, lo


def _ssd_kernel(xbc_ref, z_ref, dt_ref, gs_ref, alog_ref, dskip_ref, snw_ref, wso_ref,
                o_ref, st_ref, y_ref, dec_ref):
    tl = xbc_ref.shape[0]
    d_inner = z_ref.shape[1]
    n_blocks = d_inner // LANES
    blocks_per_group = n_blocks // N_GROUPS
    b_off = d_inner
    c_off = d_inner + N_GROUPS * D_STATE
    ct = SSD_CHUNK

    @pl.when(pl.program_id(1) == 0)
    def _():
        st_ref[...] = jnp.zeros(st_ref.shape, F32)

    li = lax.broadcasted_iota(jnp.int32, (ct, ct), 0)
    si = lax.broadcasted_iota(jnp.int32, (ct, ct), 1)
    causal = li >= si
    tri = jnp.where(causal, 1.0, 0.0).astype(BF16)
    mask_bias = jnp.where(causal, 0.0, MASK_BIAS)
    low_half = si < HEAD_DIM

    for c in range(tl // ct):
        dtc = dt_ref[c * ct:(c + 1) * ct, :]
        da = dtc * -jnp.exp(alog_ref[...])
        hi, mid, lo = _split3(da)
        acs = _dot(tri, hi) + _dot(tri, mid) + _dot(tri, lo)
        acs2 = acs * LOG2E
        acs2_t = acs2.T
        dt_t = dtc.T
        dec_ref[c, 0] = acs2
        dec_ref[c, 1] = acs2_t - jnp.log2(dt_t)
        dec_ref[c, 2] = dt_t * jnp.exp2(acs2_t[:, ct - 1:ct] - acs2_t)

    def chunk(c, carry):
        r0 = pl.multiple_of(c * ct, ct)
        rows = pl.ds(r0, ct)
        acs2 = dec_ref[c, 0]
        row2_t = dec_ref[c, 1]
        w_t = dec_ref[c, 2]
        e_last = jnp.exp2(acs2[ct - 1:ct, :])
        for g in range(N_GROUPS):
            cg_b = xbc_ref[rows, c_off + g * D_STATE:c_off + (g + 1) * D_STATE]
            bg_b = xbc_ref[rows, b_off + g * D_STATE:b_off + (g + 1) * D_STATE]
            cb_mat = lax.dot_general(cg_b, bg_b, (((1,), (1,)), ((), ())),
                                     preferred_element_type=F32)
            bg_t = bg_b.astype(F32).T
            st_g = st_ref[g]
            y_off = _dot(cg_b, st_g.astype(BF16))
            new_cols = []
            for q in range(blocks_per_group):
                blk = g * blocks_per_group + q
                h0 = blk * HEADS_PER_BLOCK
                bs = slice(blk * LANES, (blk + 1) * LANES)
                xblk = xbc_ref[rows, bs]
                m_parts, b_parts, e_cols = [], [], []
                for hh in (h0, h0 + 1):
                    colb = jnp.broadcast_to(acs2[:, hh:hh + 1], (ct, ct))
                    m_parts.append(jnp.exp2(colb - row2_t[hh:hh + 1, :] + mask_bias) * cb_mat)
                    b_parts.append(bg_t * w_t[hh:hh + 1, :])
                    e_cols.append(jnp.exp2(colb))
                x_lo = jnp.where(low_half, xblk, jnp.zeros_like(xblk))
                x_hi = jnp.where(low_half, jnp.zeros_like(xblk), xblk)
                x_bd = jnp.concatenate([x_lo, x_hi], axis=0)
                m_cat = jnp.concatenate(m_parts, axis=1).astype(BF16)
                y_diag = _dot(m_cat, x_bd)
                e_blk = jnp.where(low_half, e_cols[0], e_cols[1])
                cs = slice(q * LANES, (q + 1) * LANES)
                y_ref[rows, bs] = (y_diag + e_blk * y_off[:, cs]
                                   + dskip_ref[:, bs] * xblk.astype(F32))
                b_cat = jnp.concatenate(b_parts, axis=1).astype(BF16)
                dec = jnp.where(low_half[0:1, :], e_last[:, h0:h0 + 1], e_last[:, h0 + 1:h0 + 2])
                new_cols.append(st_g[:, cs] * dec + _dot(b_cat, x_bd))
            st_ref[g] = jnp.concatenate(new_cols, axis=1)
        return carry

    lax.fori_loop(0, tl // ct, chunk, 0, unroll=2)

    gw = d_inner // N_GROUPS
    parts = []
    for g in range(N_GROUPS):
        gsl = slice(g * gw, (g + 1) * gw)
        yg = y_ref[:, gsl] * z_ref[:, gsl].astype(F32)
        ms = jnp.mean(yg * yg, axis=-1, keepdims=True)
        parts.append((yg * lax.rsqrt(ms + NORM_EPS) * snw_ref[:, gsl]).astype(BF16))
    yn = jnp.concatenate(parts, axis=1)
    o_ref[...] = (gs_ref[...].astype(F32) * _dot(yn, wso_ref[...])).astype(BF16)


def _ssd_branch(xbc, z, dt, gs, alog_pad, dskip, snw, wso):
    bsz, seq, d = gs.shape
    d_xbc = xbc.shape[-1]
    d_inner = z.shape[-1]
    tl = SSD_ROWS
    row = lambda w: pl.BlockSpec((None, tl, w), lambda b, i: (b, i, 0))
    return pl.pallas_call(
        _ssd_kernel,
        grid=(bsz, seq // tl),
        in_specs=[row(d_xbc), row(d_inner), row(LANES), row(d),
                  _resident((1, LANES)), _resident((1, d_inner)), _resident((1, d_inner)),
                  _resident(wso.shape)],
        out_specs=row(d),
        out_shape=jax.ShapeDtypeStruct((bsz, seq, d), BF16),
        scratch_shapes=[pltpu.VMEM((N_GROUPS, D_STATE, d_inner // N_GROUPS), F32),
                        pltpu.VMEM((tl, d_inner), F32),
                        pltpu.VMEM((tl // SSD_CHUNK, 3, SSD_CHUNK, LANES), F32)],
        compiler_params=_params(2),
        name="ssd",
    )(xbc, z, dt, gs, alog_pad, dskip, snw, wso)


def _mixffn_kernel(u_ref, gc_ref, ysg_ref, x_ref, gate1_ref,
                   cw_ref, cb_ref, lnw_ref, lnb_ref, wco_ref, wout_ref,
                   shift_ref, scale_ref, gate2_ref, nw_ref, fnw_ref, win_ref, wdn_ref,
                   o_ref,
                   ext_ref, ph_ref, x1_ref, h_ref, act_ref, *, halo, slabs):
    tl, d = u_ref.shape
    taps = cw_ref.shape[0]
    ph_rows = ph_ref.shape[2]
    step = pl.program_id(1)

    @pl.when(step == 0)
    def _():
        ext_ref[0:halo, :] = jnp.zeros((halo, d), F32)

    @pl.when((step == 0) & (pl.program_id(0) == 0))
    def _():
        x1_ref[...] = jnp.zeros(x1_ref.shape, F32)
        h_ref[...] = jnp.zeros(h_ref.shape, BF16)


    first = halo - (taps - 1)
    d_ff = wdn_ref.shape[0]

    ext_ref[halo:halo + tl, :] = u_ref[...].astype(F32)

    def conv_block(cb):
        cs = slice(cb * LANES, (cb + 1) * LANES)
        for p in range(1, SUBLANES):
            ph_ref[cb, p - 1] = ext_ref[p:p + ph_rows, cs]
        out = []
        for rb in range(tl // CONV_ROW_TILE):
            r0 = rb * CONV_ROW_TILE
            acc = jnp.zeros((CONV_ROW_TILE, LANES), F32) + cb_ref[:, cs]
            for k in range(taps):
                a8, p = divmod(first + k, SUBLANES)
                lo = r0 + a8 * SUBLANES
                if p == 0:
                    src = ext_ref[lo:lo + CONV_ROW_TILE, cs]
                else:
                    src = ph_ref[cb, p - 1, lo:lo + CONV_ROW_TILE, :]
                acc = acc + cw_ref[k:k + 1, cs] * src
            out.append(acc)
        return jnp.concatenate(out, axis=0)

    conv_blocks = list(range(d // LANES))
    n_stages = 3 * len(slabs)
    bounds = [-(-len(conv_blocks) * k // n_stages) for k in range(n_stages + 1)]
    shares = [conv_blocks[bounds[k]:bounds[k + 1]] for k in range(n_stages)]
    v_blocks = []
    tok = None
    ffn_acc = None

    def conv_stage(k, tok):
        outs = []
        for cb in shares[k]:
            if tok is not None:
                _wait_on(ext_ref, slice(cb * LANES, (cb + 1) * LANES), tok)
            outs.append(conv_block(cb))
        v_blocks.extend(outs)
        return outs[-1:]

    for si, (s0, s1) in enumerate(slabs):
        w = s1 - s0
        g = _dot(h_ref[...], win_ref[:, s0:s1])
        tok = _token(g, *conv_stage(3 * si, tok))

        up = _dot(h_ref[...], win_ref[:, d_ff + s0:d_ff + s1])
        act_ref[:, 0:w] = (_silu(g) * up).astype(BF16)
        tok = _token(up, *conv_stage(3 * si + 1, tok))

        part = _dot(act_ref[:, 0:w], wdn_ref[s0:s1, :])
        ffn_acc = part if ffn_acc is None else ffn_acc + part
        tok = _token(part, *conv_stage(3 * si + 2, tok))

    x2 = x1_ref[...] + gate2_ref[...] * ffn_acc
    ms = jnp.mean(x2 * x2, axis=-1, keepdims=True)
    out = x2 * lax.rsqrt(ms + NORM_EPS) * fnw_ref[...]

    v = jnp.concatenate(v_blocks, axis=1)
    new_halo = ext_ref[tl:tl + halo, :]
    mu = jnp.mean(v, axis=-1, keepdims=True)
    cen = v - mu
    var = jnp.mean(cen * cen, axis=-1, keepdims=True)
    y = cen * lax.rsqrt(var + LN_EPS) * lnw_ref[...] + lnb_ref[...]
    y_conv = _dot(_silu(y).astype(BF16), wco_ref[...])
    merged = gc_ref[...].astype(F32) * y_conv + ysg_ref[...].astype(F32)
    x1 = x_ref[...] + gate1_ref[...] * _dot(merged.astype(BF16), wout_ref[...])
    h = _modulated_rms(x1, nw_ref[...], scale_ref[...], shift_ref[...])

    o_ref[...] = out
    ext_ref[0:halo, :] = new_halo
    x1_ref[...] = x1
    h_ref[...] = h.astype(BF16)


def _mixffn(u, gc, ysg, x, gate1, cw, cb, lnw, lnb, wco, wout,
            shift, scale, gate2, nw, fnw, win, wdn):
    bsz, seq, d = x.shape
    tl = MIXFFN_ROWS
    taps = cw.shape[0]
    halo = _round_up(taps - 1, SUBLANES)
    first = halo - (taps - 1)
    ph_rows = tl + max((first + k) // SUBLANES * SUBLANES
                       for k in range(taps) if (first + k) % SUBLANES)
    d_ff = wdn.shape[0]
    col_tiles = -(-d_ff // MXU_COLS)
    cuts = [min(-(-col_tiles * k // FFN_SLABS) * MXU_COLS, d_ff) for k in range(FFN_SLABS + 1)]
    slabs = tuple(zip(cuts[:-1], cuts[1:]))
    slab_w = max(s1 - s0 for s0, s1 in slabs)
    kern = functools.partial(_mixffn_kernel, halo=halo, slabs=slabs)
    n_tiles = seq // tl
    row = pl.BlockSpec((None, tl, d), lambda b, i: (b, jnp.minimum(i, n_tiles - 1), 0))
    out_row = pl.BlockSpec((None, tl, d), lambda b, i: (b, jnp.maximum(i - 1, 0), 0))
    vec = pl.BlockSpec((None, 1, d), lambda b, i: (b, 0, 0))
    r1 = _resident((1, d))
    return pl.pallas_call(
        kern,
        grid=(bsz, n_tiles + 1),
        in_specs=[row, row, row, row, vec,
                  _resident(cw.shape), r1, r1, r1, _resident(wco.shape), _resident(wout.shape),
                  vec, vec, vec, r1, r1, _resident(win.shape), _resident(wdn.shape)],
        out_specs=out_row,
        out_shape=jax.ShapeDtypeStruct((bsz, seq, d), F32),
        scratch_shapes=[pltpu.VMEM((tl + halo, d), F32),
                        pltpu.VMEM((d // LANES, SUBLANES - 1, ph_rows, LANES), F32),
                        pltpu.VMEM((tl, d), F32),
                        pltpu.VMEM((tl, d), BF16),
                        pltpu.VMEM((tl, slab_w), BF16)],
        compiler_params=_params(2),
        name="mixffn",
    )(u, gc, ysg, x, gate1, cw, cb, lnw, lnb, wco, wout, shift, scale, gate2, nw, fnw, win, wdn)


def _mods(c, w, b):
    bsz, d = c.shape
    mod = _ada(c, w, b)
    return [mod[:, i * d:(i + 1) * d].reshape(bsz, 1, d) for i in range(3)]


def _pad_lanes(v, width=LANES):
    return jnp.pad(v, [(0, 0)] * (v.ndim - 1) + [(0, width - v.shape[-1])])


def kernel(x, c, w_ada_mix, b_ada_mix, norm_mix_w, w_in, conv_dw_w, conv_dw_b, conv_ln_w, conv_ln_b, w_conv_out, ssm_conv_w, ssm_conv_b, dt_bias, a_log, d_skip, ssm_norm_w, w_ssm_out, w_out, w_ada_ffn, b_ada_ffn, norm_ffn_w, w_ffn_in, w_ffn_down, final_norm_w):
    bsz, seq, d = x.shape
    depth = w_in.shape[0]
    assert depth == 1, "the last kernel applies the final RMSNorm, so one layer only"
    d_conv = conv_dw_w.shape[-1]
    d_inner = ssm_norm_w.shape[-1]
    d_xbc = ssm_conv_w.shape[-1]
    n_heads = dt_bias.shape[-1]
    assert d_inner == n_heads * HEAD_DIM and d_xbc == d_inner + 2 * N_GROUPS * D_STATE
    assert n_heads <= LANES and seq % max(INPROJ_ROWS, SSD_ROWS, MIXFFN_ROWS) == 0
    row_vec = lambda v: v.reshape(1, -1).astype(F32)
    i = 0

    shift, scale, gate = _mods(c, w_ada_mix[i], b_ada_mix[i])
    w = w_in[i]
    s_dt = 2 * d_conv + d_inner + d_xbc
    w_lo = w.astype(BF16)
    w_hi = w[:, s_dt + n_heads:].astype(BF16)
    w_dt = _pad_lanes(w[:, s_dt:s_dt + n_heads]).astype(BF16)
    u, z, xbc, dt, gc, gs = _inproj(
        x, shift, scale, row_vec(norm_mix_w[i]), w_lo, w_hi, w_dt, _pad_lanes(row_vec(dt_bias[i])),
        ssm_conv_w[i], row_vec(ssm_conv_b[i]), d_conv=d_conv, d_inner=d_inner, d_xbc=d_xbc)
    dskip = row_vec(jnp.repeat(d_skip[i].astype(F32), HEAD_DIM))
    ysg = _ssd_branch(xbc, z, dt, gs, _pad_lanes(row_vec(a_log[i])), dskip,
                      row_vec(ssm_norm_w[i]), w_ssm_out[i].astype(BF16))
    shift2, scale2, gate2 = _mods(c, w_ada_ffn[i], b_ada_ffn[i])
    return _mixffn(u, gc, ysg, x, gate, conv_dw_w[i], row_vec(conv_dw_b[i]),
                   row_vec(conv_ln_w[i]), row_vec(conv_ln_b[i]), w_conv_out[i].astype(BF16),
                   w_out[i].astype(BF16), shift2, scale2, gate2, row_vec(norm_ffn_w[i]),
                   row_vec(final_norm_w), w_ffn_in[i].astype(BF16), w_ffn_down[i].astype(BF16))
```

```python
import functools
import math

import jax
import jax.numpy as jnp
from jax import lax
from jax.experimental import pallas as pl
from jax.experimental.pallas import tpu as pltpu

F32 = jnp.float32
BF16 = jnp.bfloat16

N_GROUPS = 4
D_STATE = 128
HEAD_DIM = 64
NORM_EPS = 1e-6
LN_EPS = 1e-5

LANES = 128
SUBLANES = 8
MXU_COLS = 256
SSD_CHUNK = LANES
HEADS_PER_BLOCK = LANES // HEAD_DIM
MASK_BIAS = -1e30
LOG2E = math.log2(math.e)
VMEM_LIMIT = 56 * 1024 * 1024

INPROJ_ROWS = 512
SSD_ROWS = 1024
MIXFFN_ROWS = 256
COL_TILE = 512
CONV_ROW_TILE = 64
FFN_SLABS = 3


def _sigmoid(v):
    return jax.nn.sigmoid(v)


def _silu(v):
    return v * jax.nn.sigmoid(v)


def _softplus(v):
    return jnp.maximum(v, 0.0) + jnp.log1p(jnp.exp(-jnp.abs(v)))


def _dot(a, b):
    return jnp.dot(a, b, preferred_element_type=F32)


def _resident(shape):
    nd = len(shape)
    return pl.BlockSpec(shape, lambda *_: (0,) * nd, pipeline_mode=pl.Buffered(1))


def _params(n_axes, flags=None):
    return pltpu.CompilerParams(
        dimension_semantics=("arbitrary",) * n_axes, vmem_limit_bytes=VMEM_LIMIT, flags=flags)


def _round_up(n, m):
    return -(-n // m) * m


def _token(*vals):
    tok = jnp.zeros((SUBLANES, LANES), F32)
    for val in vals:
        bits = lax.bitcast_convert_type(val[-SUBLANES:, -LANES:], jnp.int32)
        bits = lax.shift_right_logical(lax.shift_right_logical(bits, 16), 16)
        tok = tok + lax.bitcast_convert_type(bits, F32)
    return tok


def _wait_on(ref, cols, tok):
    packing = 4 // jnp.dtype(ref.dtype).itemsize
    region = ref[:, cols]
    reps = (region.shape[0] // (SUBLANES * packing), region.shape[1] // LANES)
    t = jnp.tile(jnp.concatenate([tok] * packing, axis=0).astype(ref.dtype), reps)
    ref[:, cols] = region + t


def _ada_kernel(c_ref, w_ref, b_ref, o_ref):
    c = c_ref[...]
    o_ref[...] = jnp.dot(_silu(c), w_ref[...], preferred_element_type=F32,
                         precision=lax.Precision.HIGHEST) + b_ref[...]


def _ada(c, w, b):
    bsz, d = c.shape
    n = w.shape[1]
    tn = d
    return pl.pallas_call(
        _ada_kernel,
        grid=(n // tn,),
        in_specs=[pl.BlockSpec((bsz, d), lambda j: (0, 0)),
                  pl.BlockSpec((d, tn), lambda j: (0, j)),
                  pl.BlockSpec((1, tn), lambda j: (0, j))],
        out_specs=pl.BlockSpec((bsz, tn), lambda j: (0, j)),
        out_shape=jax.ShapeDtypeStruct((bsz, n), F32),
        compiler_params=_params(1),
        name="ada",
    )(c, w, b.reshape(1, n))


def _modulated_rms(x, nw, scale, shift):
    ms = jnp.mean(x * x, axis=-1, keepdims=True)
    h = x * lax.rsqrt(ms + NORM_EPS) * nw
    return h * (1.0 + scale) + shift


def _inproj_kernel(x_ref, shift_ref, scale_ref, nw_ref, wlo_ref, whi_ref, wdt_ref, dtb_ref,
                   scw_ref, scb_ref,
                   u_ref, z_ref, xbc_ref, dt_ref, gc_ref, gs_ref,
                   h_ref, work_ref, halo_ref, *, d_conv, d_inner, d_xbc):
    tm = x_ref.shape[0]
    taps = scw_ref.shape[0]
    halo = halo_ref.shape[0]

    @pl.when(pl.program_id(1) == 0)
    def _():
        halo_ref[...] = jnp.zeros(halo_ref.shape, F32)

    h = _modulated_rms(x_ref[...], nw_ref[...], scale_ref[...], shift_ref[...])
    h_ref[...] = h.astype(BF16)

    def proj(c0):
        split = 2 * d_conv + d_inner + d_xbc
        w = wlo_ref[:, c0:c0 + COL_TILE] if c0 < split else whi_ref[:, c0 - split:c0 - split + COL_TILE]
        return _dot(h_ref[...], w)

    d_model = gc_ref.shape[-1]
    z_off = 2 * d_conv
    xbc_off = z_off + d_inner
    gc_off = xbc_off + d_xbc
    gs_off = gc_off + d_model
    first = halo - (taps - 1)

    def glu_task(c0):
        a = proj(c0)
        b = proj(d_conv + c0)
        u_ref[:, c0:c0 + COL_TILE] = (a * _sigmoid(b)).astype(BF16)

    def act_task(out_ref, act, off, c0):
        out_ref[:, c0:c0 + COL_TILE] = act(proj(off + c0)).astype(BF16)

    def xbc_task(j):
        c0 = j * COL_TILE
        cs = slice(c0, c0 + COL_TILE)
        work = work_ref.at[j % 2]
        work[0:halo, :] = halo_ref[:, cs]
        work[halo:halo + tm, :] = proj(xbc_off + c0)
        acc = jnp.zeros((tm, COL_TILE), F32) + scb_ref[:, cs]
        for k in range(taps):
            acc = acc + scw_ref[k:k + 1, cs] * work[first + k:first + k + tm, :]
        xbc_ref[:, cs] = _silu(acc).astype(BF16)
        halo_ref[:, cs] = work[tm:tm + halo, :]

    light = [functools.partial(glu_task, j * COL_TILE) for j in range(d_conv // COL_TILE)]
    light += [functools.partial(act_task, z_ref, _silu, z_off, j * COL_TILE)
              for j in range(d_inner // COL_TILE)]
    light += [functools.partial(act_task, gc_ref, _sigmoid, gc_off, j * COL_TILE)
              for j in range(d_model // COL_TILE)]
    light += [functools.partial(act_task, gs_ref, _sigmoid, gs_off, j * COL_TILE)
              for j in range(d_model // COL_TILE)]
    heavy = [functools.partial(xbc_task, j) for j in range(d_xbc // COL_TILE)]
    per_heavy = -(-len(light) // len(heavy))
    for j, task in enumerate(heavy):
        task()
        for t in light[j * per_heavy:(j + 1) * per_heavy]:
            t()
    dt_raw = _dot(h_ref[...], wdt_ref[...]) + dtb_ref[...]
    dt_ref[...] = _softplus(dt_raw)


def _inproj(x, shift, scale, nw, w_lo, w_hi, w_dt, dt_bias, scw, scb, *, d_conv, d_inner, d_xbc):
    bsz, seq, d = x.shape
    tm = INPROJ_ROWS
    halo = _round_up(scw.shape[0] - 1, SUBLANES)
    kern = functools.partial(_inproj_kernel, d_conv=d_conv, d_inner=d_inner, d_xbc=d_xbc)
    row = lambda w: pl.BlockSpec((None, tm, w), lambda b, i: (b, i, 0))
    vec = pl.BlockSpec((None, 1, d), lambda b, i: (b, 0, 0))
    outs = [(d_conv, BF16), (d_inner, BF16), (d_xbc, BF16), (LANES, F32), (d, BF16), (d, BF16)]
    return pl.pallas_call(
        kern,
        grid=(bsz, seq // tm),
        in_specs=[row(d), vec, vec, _resident((1, d)), _resident(w_lo.shape), _resident(w_hi.shape),
                  _resident(w_dt.shape), _resident((1, LANES)),
                  _resident(scw.shape), _resident((1, d_xbc))],
        out_specs=[row(w) for w, _ in outs],
        out_shape=[jax.ShapeDtypeStruct((bsz, seq, w), t) for w, t in outs],
        scratch_shapes=[pltpu.VMEM((tm, d), BF16),
                        pltpu.VMEM((2, tm + halo, COL_TILE), F32),
                        pltpu.VMEM((halo, d_xbc), F32)],
        compiler_params=_params(2),
        name="inproj",
    )(x, shift, scale, nw, w_lo, w_hi, w_dt, dt_bias, scw, scb)


def _split3(v):
    hi = v.astype(BF16)
    r1 = v - hi.astype(F32)
    mid = r1.astype(BF16)
    lo = (r1 - mid.astype(F32)).astype(BF16)
    return hi, mid, lo


def _ssd_kernel(xbc_ref, z_ref, dt_ref, gs_ref, alog_ref, dskip_ref, snw_ref, wso_ref,
                o_ref, st_ref, y_ref, dec_ref):
    tl = xbc_ref.shape[0]
    d_inner = z_ref.shape[1]
    n_blocks = d_inner // LANES
    blocks_per_group = n_blocks // N_GROUPS
    b_off = d_inner
    c_off = d_inner + N_GROUPS * D_STATE
    ct = SSD_CHUNK

    @pl.when(pl.program_id(1) == 0)
    def _():
        st_ref[...] = jnp.zeros(st_ref.shape, F32)

    li = lax.broadcasted_iota(jnp.int32, (ct, ct), 0)
    si = lax.broadcasted_iota(jnp.int32, (ct, ct), 1)
    causal = li >= si
    tri = jnp.where(causal, 1.0, 0.0).astype(BF16)
    mask_bias = jnp.where(causal, 0.0, MASK_BIAS)
    low_half = si < HEAD_DIM

    for c in range(tl // ct):
        dtc = dt_ref[c * ct:(c + 1) * ct, :]
        da = dtc * -jnp.exp(alog_ref[...])
        hi, mid, lo = _split3(da)
        acs = _dot(tri, hi) + _dot(tri, mid) + _dot(tri, lo)
        acs2 = acs * LOG2E
        acs2_t = acs2.T
        dt_t = dtc.T
        dec_ref[c, 0] = acs2
        dec_ref[c, 1] = acs2_t - jnp.log2(dt_t)
        dec_ref[c, 2] = dt_t * jnp.exp2(acs2_t[:, ct - 1:ct] - acs2_t)

    def chunk(c, carry):
        r0 = pl.multiple_of(c * ct, ct)
        rows = pl.ds(r0, ct)
        acs2 = dec_ref[c, 0]
        row2_t = dec_ref[c, 1]
        w_t = dec_ref[c, 2]
        e_last = jnp.exp2(acs2[ct - 1:ct, :])
        for g in range(N_GROUPS):
            cg_b = xbc_ref[rows, c_off + g * D_STATE:c_off + (g + 1) * D_STATE]
            bg_b = xbc_ref[rows, b_off + g * D_STATE:b_off + (g + 1) * D_STATE]
            cb_mat = lax.dot_general(cg_b, bg_b, (((1,), (1,)), ((), ())),
                                     preferred_element_type=F32)
            bg_t = bg_b.astype(F32).T
            st_g = st_ref[g]
            y_off = _dot(cg_b, st_g.astype(BF16))
            new_cols = []
            for q in range(blocks_per_group):
                blk = g * blocks_per_group + q
                h0 = blk * HEADS_PER_BLOCK
                bs = slice(blk * LANES, (blk + 1) * LANES)
                xblk = xbc_ref[rows, bs]
                m_parts, b_parts, e_cols = [], [], []
                for hh in (h0, h0 + 1):
                    colb = jnp.broadcast_to(acs2[:, hh:hh + 1], (ct, ct))
                    m_parts.append(jnp.exp2(colb - row2_t[hh:hh + 1, :] + mask_bias) * cb_mat)
                    b_parts.append(bg_t * w_t[hh:hh + 1, :])
                    e_cols.append(jnp.exp2(colb))
                x_lo = jnp.where(low_half, xblk, jnp.zeros_like(xblk))
                x_hi = jnp.where(low_half, jnp.zeros_like(xblk), xblk)
                x_bd = jnp.concatenate([x_lo, x_hi], axis=0)
                m_cat = jnp.concatenate(m_parts, axis=1).astype(BF16)
                y_diag = _dot(m_cat, x_bd)
                e_blk = jnp.where(low_half, e_cols[0], e_cols[1])
                cs = slice(q * LANES, (q + 1) * LANES)
                y_ref[rows, bs] = (y_diag + e_blk * y_off[:, cs]
                                   + dskip_ref[:, bs] * xblk.astype(F32))
                b_cat = jnp.concatenate(b_parts, axis=1).astype(BF16)
                dec = jnp.where(low_half[0:1, :], e_last[:, h0:h0 + 1], e_last[:, h0 + 1:h0 + 2])
                new_cols.append(st_g[:, cs] * dec + _dot(b_cat, x_bd))
            st_ref[g] = jnp.concatenate(new_cols, axis=1)
        return carry

    lax.fori_loop(0, tl // ct, chunk, 0, unroll=2)

    gw = d_inner // N_GROUPS
    parts = []
    for g in range(N_GROUPS):
        gsl = slice(g * gw, (g + 1) * gw)
        yg = y_ref[:, gsl] * z_ref[:, gsl].astype(F32)
        ms = jnp.mean(yg * yg, axis=-1, keepdims=True)
        parts.append((yg * lax.rsqrt(ms + NORM_EPS) * snw_ref[:, gsl]).astype(BF16))
    yn = jnp.concatenate(parts, axis=1)
    o_ref[...] = (gs_ref[...].astype(F32) * _dot(yn, wso_ref[...])).astype(BF16)


def _ssd_branch(xbc, z, dt, gs, alog_pad, dskip, snw, wso):
    bsz, seq, d = gs.shape
    d_xbc = xbc.shape[-1]
    d_inner = z.shape[-1]
    tl = SSD_ROWS
    row = lambda w: pl.BlockSpec((None, tl, w), lambda b, i: (b, i, 0))
    return pl.pallas_call(
        _ssd_kernel,
        grid=(bsz, seq // tl),
        in_specs=[row(d_xbc), row(d_inner), row(LANES), row(d),
                  _resident((1, LANES)), _resident((1, d_inner)), _resident((1, d_inner)),
                  _resident(wso.shape)],
        out_specs=row(d),
        out_shape=jax.ShapeDtypeStruct((bsz, seq, d), BF16),
        scratch_shapes=[pltpu.VMEM((N_GROUPS, D_STATE, d_inner // N_GROUPS), F32),
                        pltpu.VMEM((tl, d_inner), F32),
                        pltpu.VMEM((tl // SSD_CHUNK, 3, SSD_CHUNK, LANES), F32)],
        compiler_params=_params(2),
        name="ssd",
    )(xbc, z, dt, gs, alog_pad, dskip, snw, wso)


def _mixffn_kernel(u_ref, gc_ref, ysg_ref, x_ref, gate1_ref,
                   cw_ref, cb_ref, lnw_ref, lnb_ref, wco_ref, wout_ref,
                   shift_ref, scale_ref, gate2_ref, nw_ref, fnw_ref, win_ref, wdn_ref,
                   o_ref,
                   ext_ref, ph_ref, x1_ref, h_ref, act_ref, *, halo, slabs):
    tl, d = u_ref.shape
    taps = cw_ref.shape[0]
    ph_rows = ph_ref.shape[2]
    step = pl.program_id(1)

    @pl.when(step == 0)
    def _():
        ext_ref[0:halo, :] = jnp.zeros((halo, d), F32)

    @pl.when((step == 0) & (pl.program_id(0) == 0))
    def _():
        x1_ref[...] = jnp.zeros(x1_ref.shape, F32)
        h_ref[...] = jnp.zeros(h_ref.shape, BF16)


    first = halo - (taps - 1)
    d_ff = wdn_ref.shape[0]

    ext_ref[halo:halo + tl, :] = u_ref[...].astype(F32)

    def conv_block(cb):
        cs = slice(cb * LANES, (cb + 1) * LANES)
        for p in range(1, SUBLANES):
            ph_ref[cb, p - 1] = ext_ref[p:p + ph_rows, cs]
        out = []
        for rb in range(tl // CONV_ROW_TILE):
            r0 = rb * CONV_ROW_TILE
            acc = jnp.zeros((CONV_ROW_TILE, LANES), F32) + cb_ref[:, cs]
            for k in range(taps):
                a8, p = divmod(first + k, SUBLANES)
                lo = r0 + a8 * SUBLANES
                if p == 0:
                    src = ext_ref[lo:lo + CONV_ROW_TILE, cs]
                else:
                    src = ph_ref[cb, p - 1, lo:lo + CONV_ROW_TILE, :]
                acc = acc + cw_ref[k:k + 1, cs] * src
            out.append(acc)
        return jnp.concatenate(out, axis=0)

    conv_blocks = list(range(d // LANES))
    n_stages = 3 * len(slabs)
    bounds = [-(-len(conv_blocks) * k // n_stages) for k in range(n_stages + 1)]
    shares = [conv_blocks[bounds[k]:bounds[k + 1]] for k in range(n_stages)]
    v_blocks = []
    tok = None
    ffn_acc = None

    def conv_stage(k, tok):
        outs = []
        for cb in shares[k]:
            if tok is not None:
                _wait_on(ext_ref, slice(cb * LANES, (cb + 1) * LANES), tok)
            outs.append(conv_block(cb))
        v_blocks.extend(outs)
        return outs[-1:]

    for si, (s0, s1) in enumerate(slabs):
        w = s1 - s0
        g = _dot(h_ref[...], win_ref[:, s0:s1])
        tok = _token(g, *conv_stage(3 * si, tok))

        up = _dot(h_ref[...], win_ref[:, d_ff + s0:d_ff + s1])
        act_ref[:, 0:w] = (_silu(g) * up).astype(BF16)
        tok = _token(up, *conv_stage(3 * si + 1, tok))

        part = _dot(act_ref[:, 0:w], wdn_ref[s0:s1, :])
        ffn_acc = part if ffn_acc is None else ffn_acc + part
        tok = _token(part, *conv_stage(3 * si + 2, tok))

    x2 = x1_ref[...] + gate2_ref[...] * ffn_acc
    ms = jnp.mean(x2 * x2, axis=-1, keepdims=True)
    out = x2 * lax.rsqrt(ms + NORM_EPS) * fnw_ref[...]

    v = jnp.concatenate(v_blocks, axis=1)
    new_halo = ext_ref[tl:tl + halo, :]
    mu = jnp.mean(v, axis=-1, keepdims=True)
    cen = v - mu
    var = jnp.mean(cen * cen, axis=-1, keepdims=True)
    y = cen * lax.rsqrt(var + LN_EPS) * lnw_ref[...] + lnb_ref[...]
    y_conv = _dot(_silu(y).astype(BF16), wco_ref[...])
    merged = gc_ref[...].astype(F32) * y_conv + ysg_ref[...].astype(F32)
    x1 = x_ref[...] + gate1_ref[...] * _dot(merged.astype(BF16), wout_ref[...])
    h = _modulated_rms(x1, nw_ref[...], scale_ref[...], shift_ref[...])

    o_ref[...] = out
    ext_ref[0:halo, :] = new_halo
    x1_ref[...] = x1
    h_ref[...] = h.astype(BF16)


def _mixffn(u, gc, ysg, x, gate1, cw, cb, lnw, lnb, wco, wout,
            shift, scale, gate2, nw, fnw, win, wdn):
    bsz, seq, d = x.shape
    tl = MIXFFN_ROWS
    taps = cw.shape[0]
    halo = _round_up(taps - 1, SUBLANES)
    first = halo - (taps - 1)
    ph_rows = tl + max((first + k) // SUBLANES * SUBLANES
                       for k in range(taps) if (first + k) % SUBLANES)
    d_ff = wdn.shape[0]
    col_tiles = -(-d_ff // MXU_COLS)
    cuts = [min(-(-col_tiles * k // FFN_SLABS) * MXU_COLS, d_ff) for k in range(FFN_SLABS + 1)]
    slabs = tuple(zip(cuts[:-1], cuts[1:]))
    slab_w = max(s1 - s0 for s0, s1 in slabs)
    kern = functools.partial(_mixffn_kernel, halo=halo, slabs=slabs)
    n_tiles = seq // tl
    row = pl.BlockSpec((None, tl, d), lambda b, i: (b, jnp.minimum(i, n_tiles - 1), 0))
    out_row = pl.BlockSpec((None, tl, d), lambda b, i: (b, jnp.maximum(i - 1, 0), 0))
    vec = pl.BlockSpec((None, 1, d), lambda b, i: (b, 0, 0))
    r1 = _resident((1, d))
    return pl.pallas_call(
        kern,
        grid=(bsz, n_tiles + 1),
        in_specs=[row, row, row, row, vec,
                  _resident(cw.shape), r1, r1, r1, _resident(wco.shape), _resident(wout.shape),
                  vec, vec, vec, r1, r1, _resident(win.shape), _resident(wdn.shape)],
        out_specs=out_row,
        out_shape=jax.ShapeDtypeStruct((bsz, seq, d), F32),
        scratch_shapes=[pltpu.VMEM((tl + halo, d), F32),
                        pltpu.VMEM((d // LANES, SUBLANES - 1, ph_rows, LANES), F32),
                        pltpu.VMEM((tl, d), F32),
                        pltpu.VMEM((tl, d), BF16),
                        pltpu.VMEM((tl, slab_w), BF16)],
        compiler_params=_params(2),
        name="mixffn",
    )(u, gc, ysg, x, gate1, cw, cb, lnw, lnb, wco, wout, shift, scale, gate2, nw, fnw, win, wdn)


def _mods(c, w, b):
    bsz, d = c.shape
    mod = _ada(c, w, b)
    return [mod[:, i * d:(i + 1) * d].reshape(bsz, 1, d) for i in range(3)]


def _pad_lanes(v, width=LANES):
    return jnp.pad(v, [(0, 0)] * (v.ndim - 1) + [(0, width - v.shape[-1])])


def kernel(x, c, w_ada_mix, b_ada_mix, norm_mix_w, w_in, conv_dw_w, conv_dw_b, conv_ln_w, conv_ln_b, w_conv_out, ssm_conv_w, ssm_conv_b, dt_bias, a_log, d_skip, ssm_norm_w, w_ssm_out, w_out, w_ada_ffn, b_ada_ffn, norm_ffn_w, w_ffn_in, w_ffn_down, final_norm_w):
    bsz, seq, d = x.shape
    depth = w_in.shape[0]
    assert depth == 1, "the last kernel applies the final RMSNorm, so one layer only"
    d_conv = conv_dw_w.shape[-1]
    d_inner = ssm_norm_w.shape[-1]
    d_xbc = ssm_conv_w.shape[-1]
    n_heads = dt_bias.shape[-1]
    assert d_inner == n_heads * HEAD_DIM and d_xbc == d_inner + 2 * N_GROUPS * D_STATE
    assert n_heads <= LANES and seq % max(INPROJ_ROWS, SSD_ROWS, MIXFFN_ROWS) == 0
    row_vec = lambda v: v.reshape(1, -1).astype(F32)
    i = 0

    shift, scale, gate = _mods(c, w_ada_mix[i], b_ada_mix[i])
    w = w_in[i]
    s_dt = 2 * d_conv + d_inner + d_xbc
    w_lo = w.astype(BF16)
    w_hi = w[:, s_dt + n_heads:].astype(BF16)
    w_dt = _pad_lanes(w[:, s_dt:s_dt + n_heads]).astype(BF16)
    u, z, xbc, dt, gc, gs = _inproj(
        x, shift, scale, row_vec(norm_mix_w[i]), w_lo, w_hi, w_dt, _pad_lanes(row_vec(dt_bias[i])),
        ssm_conv_w[i], row_vec(ssm_conv_b[i]), d_conv=d_conv, d_inner=d_inner, d_xbc=d_xbc)
    dskip = row_vec(jnp.repeat(d_skip[i].astype(F32), HEAD_DIM))
    ysg = _ssd_branch(xbc, z, dt, gs, _pad_lanes(row_vec(a_log[i])), dskip,
                      row_vec(ssm_norm_w[i]), w_ssm_out[i].astype(BF16))
    shift2, scale2, gate2 = _mods(c, w_ada_ffn[i], b_ada_ffn[i])
    return _mixffn(u, gc, ysg, x, gate, conv_dw_w[i], row_vec(conv_dw_b[i]),
                   row_vec(conv_ln_w[i]), row_vec(conv_ln_b[i]), w_conv_out[i].astype(BF16),
                   w_out[i].astype(BF16), shift2, scale2, gate2, row_vec(norm_ffn_w[i]),
                   row_vec(final_norm_w), w_ffn_in[i].astype(BF16), w_ffn_down[i].astype(BF16))
```
